```python
import math, functools
import jax, jax.numpy as jnp
from jax import lax
import numpy as np

D_MODEL = 1024
BATCH = 16
SEQ = 4096
DEPTH = 1
DEC_BATCH = 128
DEC_SEQ = 4
PAST_LEN = 8192
PAGE_SIZE = 128

DA_HEADS = 4
DA_DH = 64
DA_QK = 2 * DA_DH
DA_DV = 2 * DA_DH
DA_WIDTH = DA_HEADS * DA_DV
ROT_DIM = DA_DH // 4
ROPE_THETA = 500000.0
Q_BLOCK = 128
DN_HEADS = 4
DN_DH = 128
DN_WIDTH = DN_HEADS * DN_DH
CONV_W = 4
DN_CHUNK = 64
MIX_WIDTH = DA_WIDTH + DN_WIDTH
IN_SPLITS = (DA_HEADS * DA_QK, DA_HEADS * DA_QK, DA_WIDTH, 3 * DN_WIDTH, DN_HEADS, DN_HEADS, DN_WIDTH)
IN_COLS = sum(IN_SPLITS)
N_MEM = 256
MEM_HEADS = 4
MEM_DH = 128
MEM_WIDTH = MEM_HEADS * MEM_DH
N_EXPERTS = 256
TOP_K = 8
N_GROUPS = 8
TOPK_GROUPS = 4
D_EXPERT = 256
D_SHARED = 256
ROUTED_SCALE = 2.5
EXPERT_BLOCK = 128
DEEPNORM_ALPHA = (2 * DEPTH) ** 0.25
DEEPNORM_BETA = (8 * DEPTH) ** -0.25
LN_EPS = 1e-5
NEG = -1e30

kernel_name = "hybrid_diffattn_gdn_moe_decode_step"


def _layernorm(x, g, b):
    xf = x.astype(jnp.float32)
    mu = jnp.mean(xf, -1, keepdims=True)
    var = jnp.mean(jnp.square(xf - mu), -1, keepdims=True)
    return ((xf - mu) * lax.rsqrt(var + LN_EPS) * g + b).astype(x.dtype)


def _rmsnorm(x, w, eps):
    xf = x.astype(jnp.float32)
    return (xf * lax.rsqrt(jnp.mean(xf * xf, -1, keepdims=True) + eps) * w).astype(x.dtype)


def _l2norm(x):
    xf = x.astype(jnp.float32)
    return xf * lax.rsqrt(jnp.sum(xf * xf, -1, keepdims=True) + 1e-6)


def _rope(x, pos):
    half = ROT_DIM // 2
    inv = ROPE_THETA ** (-jnp.arange(0, ROT_DIM, 2, dtype=jnp.float32) / ROT_DIM)
    ang = pos.astype(jnp.float32)[:, None] * inv
    cos = jnp.cos(ang)[:, None, None, :]
    sin = jnp.sin(ang)[:, None, None, :]
    xr = x[..., :ROT_DIM].astype(jnp.float32)
    x1, x2 = xr[..., :half], xr[..., half:]
    rot = jnp.concatenate([x1 * cos - x2 * sin, x2 * cos + x1 * sin], -1)
    return jnp.concatenate([rot.astype(x.dtype), x[..., ROT_DIM:]], -1)


def _diff_attn_prompt(q, k, v, lam):
    B, S = q.shape[:2]
    scale = DA_DH ** -0.5
    kpos = jnp.arange(S)
    vf = v.astype(jnp.float32)

    def block(i):
        q0 = i * Q_BLOCK
        qb = lax.dynamic_slice_in_dim(q, q0, Q_BLOCK, axis=1)
        s = jnp.einsum('bqhcd,bkhcd->bhcqk', qb, k, preferred_element_type=jnp.float32) * scale
        qpos = q0 + jnp.arange(Q_BLOCK)
        mask = kpos[None, :] <= qpos[:, None]
        p = jax.nn.softmax(jnp.where(mask, s, NEG), axis=-1)
        a = p[:, :, 0] - lam * p[:, :, 1]
        return jnp.einsum('bhqk,bkhd->bqhd', a, vf)

    o = lax.map(block, jnp.arange(S // Q_BLOCK))
    return jnp.moveaxis(o, 0, 1).reshape(B, S, DA_HEADS, DA_DV)


def _softmax_update(carry, s, vals):
    m, l, acc = carry
    m_new = jnp.maximum(m, jnp.max(s, -1))
    corr = jnp.exp(m - m_new)
    p = jnp.exp(s - m_new[..., None])
    l = l * corr + jnp.sum(p, -1)
    acc = acc * corr[..., None] + jnp.einsum('bhctk,bkhd->bhctd', p, vals.astype(jnp.float32))
    return (m_new, l, acc)


def _diff_attn_sample(q, k, v, lam, cache_k, cache_v, page_table):
    B, T = q.shape[:2]
    page = cache_k.shape[1]
    qf = q.astype(jnp.float32) * DA_DH ** -0.5

    def page_step(carry, ids):
        kp = cache_k[ids].reshape(B, page, DA_HEADS, 2, DA_DH).astype(jnp.float32)
        vp = cache_v[ids]
        s = jnp.einsum('bthcd,bkhcd->bhctk', qf, kp)
        return _softmax_update(carry, s, vp), None

    init = (jnp.full((B, DA_HEADS, 2, T), NEG, jnp.float32),
            jnp.zeros((B, DA_HEADS, 2, T), jnp.float32),
            jnp.zeros((B, DA_HEADS, 2, T, DA_DV), jnp.float32))
    carry, _ = lax.scan(page_step, init, page_table.T)
    s = jnp.einsum('bthcd,bkhcd->bhctk', qf, k.astype(jnp.float32))
    mask = jnp.arange(T)[None, :] <= jnp.arange(T)[:, None]
    m, l, acc = _softmax_update(carry, jnp.where(mask, s, NEG), v)
    o = acc / l[..., None]
    o = o[:, :, 0] - lam * o[:, :, 1]
    return jnp.transpose(o, (0, 2, 1, 3))


def _gated_delta(q, k, v, beta, g, state0):
    B, T, H, DK = q.shape
    DV = v.shape[-1]
    C = DN_CHUNK if T % DN_CHUNK == 0 else T
    NC = T // C

    def chunks(a):
        a = a.astype(jnp.float32).reshape((B, NC, C, H) + a.shape[3:])
        return jnp.moveaxis(a, 3, 2)

    q, k, v, beta, g = chunks(q), chunks(k), chunks(v), chunks(beta), chunks(g)
    G = jnp.cumsum(g, -1)
    incl = jnp.tril(jnp.ones((C, C), bool))
    strict = jnp.tril(jnp.ones((C, C), bool), -1)
    D = jnp.exp(jnp.where(incl, G[..., :, None] - G[..., None, :], -jnp.inf))
    kk = jnp.einsum('...id,...jd->...ij', k, k)
    L = jnp.where(strict, beta[..., :, None] * kk * D, 0.0) + jnp.eye(C, dtype=jnp.float32)
    W = lax.linalg.triangular_solve(L, beta[..., None] * v, left_side=True, lower=True, unit_diagonal=True)
    Y = lax.linalg.triangular_solve(L, (beta * jnp.exp(G))[..., None] * k, left_side=True, lower=True,
                                    unit_diagonal=True)
    P = jnp.einsum('...id,...jd->...ij', q, k) * D
    qg = q * jnp.exp(G)[..., None]
    kg = k * jnp.exp(G[..., -1:] - G)[..., None]
    glast = jnp.exp(G[..., -1])

    def step(S, xs):
        Wc, Yc, Pc, qgc, kgc, glc = xs
        U = Wc - Yc @ S
        O = qgc @ S + Pc @ U
        S = glc[..., None, None] * S + jnp.swapaxes(kgc, -1, -2) @ U
        return S, O

    xs = tuple(jnp.moveaxis(a, 1, 0) for a in (W, Y, P, qg, kg, glast))
    S, O = lax.scan(step, state0.astype(jnp.float32), xs)
    O = jnp.transpose(O, (1, 0, 3, 2, 4)).reshape(B, T, H, DV)
    return O, S.astype(state0.dtype)


def _token_mixer(x, pos, attend, conv0, state0, lw, lam_init):
    B, T, _ = x.shape
    cuts = [int(c) for c in np.cumsum(IN_SPLITS)[:-1]]
    da_q, da_k, da_v, dn_qkv, dn_a, dn_b, dn_z = jnp.split(x @ lw['w_in'], cuts, axis=-1)
    q = _rope(da_q.reshape(B, T, DA_HEADS, 2, DA_DH), pos)
    k = _rope(da_k.reshape(B, T, DA_HEADS, 2, DA_DH), pos)
    v = da_v.reshape(B, T, DA_HEADS, DA_DV)
    lam = (jnp.exp(jnp.sum((lw['da_lambda_q1'] * lw['da_lambda_k1']).astype(jnp.float32)))
           - jnp.exp(jnp.sum((lw['da_lambda_q2'] * lw['da_lambda_k2']).astype(jnp.float32))) + lam_init)
    o_da = attend(q, k, v, lam)
    o_da = (_rmsnorm(o_da, lw['da_norm_w'], LN_EPS) * (1.0 - lam_init)).astype(x.dtype).reshape(B, T, DA_WIDTH)
    xp = jnp.concatenate([conv0.astype(x.dtype), dn_qkv], axis=1)
    cw = lw['dn_conv_w']
    conv = xp[:, 0:T] * cw[0]
    for i in range(1, CONV_W):
        conv = conv + xp[:, i:i + T] * cw[i]
    conv = jax.nn.silu(conv)
    new_conv = xp[:, T:]
    dq, dk, dv = jnp.split(conv, 3, axis=-1)
    dq = _l2norm(dq.reshape(B, T, DN_HEADS, DN_DH)) * DN_DH ** -0.5
    dk = _l2norm(dk.reshape(B, T, DN_HEADS, DN_DH))
    dv = dv.reshape(B, T, DN_HEADS, DN_DH)
    beta = jax.nn.sigmoid(dn_b.astype(jnp.float32))
    g = -jnp.exp(lw['dn_a_log'].astype(jnp.float32)) * jax.nn.softplus(
        dn_a.astype(jnp.float32) + lw['dn_dt_bias'].astype(jnp.float32))
    o_dn, new_state = _gated_delta(dq, dk, dv, beta, g, state0)
    z = dn_z.reshape(B, T, DN_HEADS, DN_DH).astype(jnp.float32)
    o_dn = (_rmsnorm(o_dn, lw['dn_norm_w'], 1e-6) * jax.nn.silu(z)).astype(x.dtype).reshape(B, T, DN_WIDTH)
    out = jnp.concatenate([o_da, o_dn], -1) @ lw['w_out']
    return out, k.reshape(B, T, DA_HEADS, DA_QK), v, new_state, new_conv


def _mem_kv(mem, w_ckv):
    B, M, _ = mem.shape
    mk, mv = jnp.split(mem @ w_ckv, 2, axis=-1)
    return mk.reshape(B, M, MEM_HEADS, MEM_DH), mv.reshape(B, M, MEM_HEADS, MEM_DH)


def _cross_attn(x, mem_k, mem_v, w_cq, w_co):
    B, T, _ = x.shape
    q = (x @ w_cq).reshape(B, T, MEM_HEADS, MEM_DH)
    s = jnp.einsum('bthd,bmhd->bhtm', q, mem_k, preferred_element_type=jnp.float32) * MEM_DH ** -0.5
    p = jax.nn.softmax(s, axis=-1)
    o = jnp.einsum('bhtm,bmhd->bthd', p, mem_v.astype(jnp.float32)).reshape(B, T, MEM_WIDTH)
    return o.astype(x.dtype) @ w_co


def _routed_experts(xf, idx, gate, w_gu, w_down):
    n_tok, d = xf.shape
    n_assign = n_tok * TOP_K
    flat_e = idx.reshape(-1)
    flat_g = gate.reshape(-1)
    order = jnp.argsort(flat_e)
    sorted_e = flat_e[order]
    counts = jnp.bincount(flat_e, length=N_EXPERTS)
    padded = (counts + EXPERT_BLOCK - 1) // EXPERT_BLOCK * EXPERT_BLOCK
    pad_end = jnp.cumsum(padded)
    pad_start = pad_end - padded
    start = jnp.cumsum(counts) - counts
    dest = pad_start[sorted_e] + jnp.arange(n_assign) - start[sorted_e]
    n_blocks = -(-(n_assign + N_EXPERTS * (EXPERT_BLOCK - 1)) // EXPERT_BLOCK)
    n_rows = n_blocks * EXPERT_BLOCK
    row_tok = jnp.full((n_rows,), n_tok, jnp.int32).at[dest].set((order // TOP_K).astype(jnp.int32))
    row_gate = jnp.zeros((n_rows,), xf.dtype).at[dest].set(flat_g[order].astype(xf.dtype))
    block_exp = jnp.minimum(jnp.searchsorted(pad_end, jnp.arange(n_blocks) * EXPERT_BLOCK, side='right'),
                            N_EXPERTS - 1)
    x_pad = jnp.concatenate([xf, jnp.zeros((1, d), xf.dtype)], 0)

    def body(acc, xs):
        tok, rg, e = xs
        xb = x_pad[tok]
        gg, uu = jnp.split(xb @ w_gu[e], 2, axis=-1)
        yb = (jax.nn.silu(gg) * uu) @ w_down[e]
        return acc.at[tok].add(yb * rg[:, None]), None

    acc, _ = lax.scan(body, jnp.zeros((n_tok + 1, d), xf.dtype),
                      (row_tok.reshape(n_blocks, EXPERT_BLOCK), row_gate.reshape(n_blocks, EXPERT_BLOCK), block_exp))
    return acc[:n_tok]


def _moe(x, w_router, router_bias, w_exp_gu, w_exp_down, w_sh_gu, w_sh_down):
    B, T, D = x.shape
    xf = x.reshape(B * T, D)
    n_tok = B * T
    scores = jax.nn.sigmoid((xf @ w_router).astype(jnp.float32))
    biased = scores + router_bias.astype(jnp.float32)
    grp = biased.reshape(n_tok, N_GROUPS, N_EXPERTS // N_GROUPS)
    grp_score = jnp.sum(lax.top_k(grp, 2)[0], -1)
    _, top_g = lax.top_k(grp_score, TOPK_GROUPS)
    gmask = jnp.any(top_g[..., None] == jnp.arange(N_GROUPS), axis=-2)
    emask = jnp.repeat(gmask, N_EXPERTS // N_GROUPS, axis=-1)
    _, idx = lax.top_k(jnp.where(emask, biased, -jnp.inf), TOP_K)
    gate = jnp.take_along_axis(scores, idx, axis=-1)
    gate = gate / jnp.sum(gate, -1, keepdims=True) * ROUTED_SCALE
    routed = _routed_experts(xf, idx, gate, w_exp_gu, w_exp_down)
    gs, us = jnp.split(xf @ w_sh_gu, 2, axis=-1)
    shared = (jax.nn.silu(gs) * us) @ w_sh_down
    return (routed + shared).reshape(B, T, D)


def _layer(x, pos, attend, conv0, state0, mem_k, mem_v, lw, lam_init):
    h, k_rows, v_rows, state, conv = _token_mixer(x, pos, attend, conv0, state0, lw, lam_init)
    x = _layernorm(DEEPNORM_ALPHA * x + h, lw['ln1_g'], lw['ln1_b'])
    x = _layernorm(DEEPNORM_ALPHA * x + _cross_attn(x, mem_k, mem_v, lw['w_cq'], lw['w_co']),
                   lw['ln2_g'], lw['ln2_b'])
    x = _layernorm(DEEPNORM_ALPHA * x + _moe(x, lw['w_router'], lw['router_bias'], lw['w_exp_gu'],
                                             lw['w_exp_down'], lw['w_sh_gu'], lw['w_sh_down']),
                   lw['ln3_g'], lw['ln3_b'])
    return x, k_rows, v_rows, state, conv


def setup_inputs(seed: int = 0) -> dict:
    key = jax.random.key(seed)
    ks = iter(jax.random.split(key, 48))
    f32 = jnp.float32

    def nrm(shape, scale=1.0):
        return jax.random.normal(next(ks), shape, f32) * scale

    def gain(shape):
        return 1.0 + nrm(shape, 0.02)

    n_pages = PAST_LEN // PAGE_SIZE
    n_used = DEC_BATCH * n_pages
    n_pool = n_used + max(1, n_used // 4)
    page_table = jax.random.permutation(next(ks), n_pool)[:n_used].astype(jnp.int32).reshape(DEC_BATCH, n_pages)
    d_in = D_MODEL ** -0.5
    return {
        'x_prompt': nrm((BATCH, SEQ, D_MODEL)),
        'x_sample': nrm((DEC_BATCH, DEC_SEQ, D_MODEL)),
        'cache_attn_k': nrm((DEPTH, n_pool, PAGE_SIZE, DA_HEADS, DA_QK)),
        'cache_attn_v': nrm((DEPTH, n_pool, PAGE_SIZE, DA_HEADS, DA_DV)),
        'cache_mem_k': nrm((DEPTH, DEC_BATCH, N_MEM, MEM_HEADS, MEM_DH)),
        'cache_mem_v': nrm((DEPTH, DEC_BATCH, N_MEM, MEM_HEADS, MEM_DH)),
        'state_dn': nrm((DEPTH, DEC_BATCH, DN_HEADS, DN_DH, DN_DH), 0.05),
        'state_dn_conv': nrm((DEPTH, DEC_BATCH, CONV_W - 1, 3 * DN_WIDTH)),
        'page_table': page_table,
        'mem_prompt': nrm((BATCH, N_MEM, D_MODEL)),
        'w_in': nrm((DEPTH, D_MODEL, IN_COLS), d_in),
        'da_lambda_q1': nrm((DEPTH, DA_DH), 0.1),
        'da_lambda_k1': nrm((DEPTH, DA_DH), 0.1),
        'da_lambda_q2': nrm((DEPTH, DA_DH), 0.1),
        'da_lambda_k2': nrm((DEPTH, DA_DH), 0.1),
        'da_norm_w': gain((DEPTH, DA_DV)),
        'dn_conv_w': nrm((DEPTH, CONV_W, 3 * DN_WIDTH), CONV_W ** -0.5),
        'dn_a_log': jnp.log(jax.random.uniform(next(ks), (DEPTH, DN_HEADS), f32, 1.0, 16.0)),
        'dn_dt_bias': nrm((DEPTH, DN_HEADS), 0.1),
        'dn_norm_w': gain((DEPTH, DN_DH)),
        'w_out': nrm((DEPTH, MIX_WIDTH, D_MODEL), MIX_WIDTH ** -0.5 * DEEPNORM_BETA),
        'ln1_g': gain((DEPTH, D_MODEL)),
        'ln1_b': nrm((DEPTH, D_MODEL), 0.02),
        'w_cq': nrm((DEPTH, D_MODEL, MEM_WIDTH), d_in),
        'w_ckv': nrm((DEPTH, D_MODEL, 2 * MEM_WIDTH), d_in),
        'w_co': nrm((DEPTH, MEM_WIDTH, D_MODEL), MEM_WIDTH ** -0.5 * DEEPNORM_BETA),
        'ln2_g': gain((DEPTH, D_MODEL)),
        'ln2_b': nrm((DEPTH, D_MODEL), 0.02),
        'w_router': nrm((DEPTH, D_MODEL, N_EXPERTS), d_in),
        'router_bias': nrm((DEPTH, N_EXPERTS), 0.01),
        'w_exp_gu': nrm((DEPTH, N_EXPERTS, D_MODEL, 2 * D_EXPERT), d_in),
        'w_exp_down': nrm((DEPTH, N_EXPERTS, D_EXPERT, D_MODEL), D_EXPERT ** -0.5 * DEEPNORM_BETA),
        'w_sh_gu': nrm((DEPTH, D_MODEL, 2 * D_SHARED), d_in),
        'w_sh_down': nrm((DEPTH, D_SHARED, D_MODEL), D_SHARED ** -0.5 * DEEPNORM_BETA),
        'ln3_g': gain((DEPTH, D_MODEL)),
        'ln3_b': nrm((DEPTH, D_MODEL), 0.02),
    }


def reference(x_prompt, x_sample, cache_attn_k, cache_attn_v, cache_mem_k, cache_mem_v, state_dn, state_dn_conv,
              page_table, mem_prompt, w_in, da_lambda_q1, da_lambda_k1, da_lambda_q2, da_lambda_k2, da_norm_w,
              dn_conv_w, dn_a_log, dn_dt_bias, dn_norm_w, w_out, ln1_g, ln1_b, w_cq, w_ckv, w_co, ln2_g, ln2_b,
              w_router, router_bias, w_exp_gu, w_exp_down, w_sh_gu, w_sh_down, ln3_g, ln3_b):
    Bp, Sp, _ = x_prompt.shape
    Bs, Ts, _ = x_sample.shape
    past = page_table.shape[1] * cache_attn_k.shape[2]
    pos_p = jnp.arange(Sp, dtype=jnp.int32)
    pos_s = past + jnp.arange(Ts, dtype=jnp.int32)
    yp, ys = x_prompt, x_sample
    kp_l, vp_l, ks_l, vs_l, sp_l, ss_l, cp_l, cs_l, mk_l, mv_l = ([] for _ in range(10))
    for l in range(DEPTH):
        lw = {'w_in': w_in[l], 'da_lambda_q1': da_lambda_q1[l], 'da_lambda_k1': da_lambda_k1[l],
              'da_lambda_q2': da_lambda_q2[l], 'da_lambda_k2': da_lambda_k2[l], 'da_norm_w': da_norm_w[l],
              'dn_conv_w': dn_conv_w[l], 'dn_a_log': dn_a_log[l], 'dn_dt_bias': dn_dt_bias[l],
              'dn_norm_w': dn_norm_w[l], 'w_out': w_out[l], 'ln1_g': ln1_g[l], 'ln1_b': ln1_b[l],
              'w_cq': w_cq[l], 'w_co': w_co[l], 'ln2_g': ln2_g[l], 'ln2_b': ln2_b[l],
              'w_router': w_router[l], 'router_bias': router_bias[l], 'w_exp_gu': w_exp_gu[l],
              'w_exp_down': w_exp_down[l], 'w_sh_gu': w_sh_gu[l], 'w_sh_down': w_sh_down[l],
              'ln3_g': ln3_g[l], 'ln3_b': ln3_b[l]}
        lam_init = 0.8 - 0.6 * math.exp(-0.3 * l)
        mk_p, mv_p = _mem_kv(mem_prompt, w_ckv[l])
        conv0 = jnp.zeros((Bp, CONV_W - 1, 3 * DN_WIDTH), x_prompt.dtype)
        st0 = jnp.zeros((Bp, DN_HEADS, DN_DH, DN_DH), x_prompt.dtype)
        yp, kp, vp, sp, cp = _layer(yp, pos_p, _diff_attn_prompt, conv0, st0, mk_p, mv_p, lw, lam_init)
        attend_s = functools.partial(_diff_attn_sample, cache_k=cache_attn_k[l], cache_v=cache_attn_v[l],
                                     page_table=page_table)
        ys, ks_, vs_, ss_, cs_ = _layer(ys, pos_s, attend_s, state_dn_conv[l], state_dn[l], cache_mem_k[l],
                                        cache_mem_v[l], lw, lam_init)
        kp_l.append(kp); vp_l.append(vp); ks_l.append(ks_); vs_l.append(vs_)
        sp_l.append(sp); ss_l.append(ss_); cp_l.append(cp); cs_l.append(cs_)
        mk_l.append(mk_p); mv_l.append(mv_p)
    return (yp, ys, jnp.stack(kp_l), jnp.stack(vp_l), jnp.stack(ks_l), jnp.stack(vs_l), jnp.stack(sp_l),
            jnp.stack(ss_l), jnp.stack(cp_l), jnp.stack(cs_l), jnp.stack(mk_l), jnp.stack(mv_l))
```

```python
import functools
import math

import jax
import jax.numpy as jnp
import numpy as np
from jax import lax
from jax.experimental import pallas as pl
from jax.experimental.pallas import tpu as pltpu

D_MODEL = 1024
DA_HEADS = 4
DA_DH = 64
DA_QK = 2 * DA_DH
DA_DV = 2 * DA_DH
DA_WIDTH = DA_HEADS * DA_DV
ROT_DIM = DA_DH // 4
ROPE_THETA = 500000.0
Q_BLOCK = 128
DN_HEADS = 4
DN_DH = 128
DN_WIDTH = DN_HEADS * DN_DH
CONV_W = 4
DN_CHUNK = 64
IN_SPLITS = (DA_HEADS * DA_QK, DA_HEADS * DA_QK, DA_WIDTH, 3 * DN_WIDTH, DN_HEADS, DN_HEADS, DN_WIDTH)
MEM_HEADS = 4
MEM_DH = 128
MEM_WIDTH = MEM_HEADS * MEM_DH
N_EXPERTS = 256
TOP_K = 8
N_GROUPS = 8
TOPK_GROUPS = 4
ROUTED_SCALE = 2.5
EXPERT_BLOCK = 128
LN_EPS = 1e-5
NEG = -1e30

V7X_VMEM_LIMIT_BYTES = 48 * 1024 * 1024


def _mm_body(x_ref, w_ref, o_ref):
    o_ref[...] = jnp.dot(x_ref[...].astype(jnp.bfloat16), w_ref[...], preferred_element_type=jnp.float32)


def _matmul(x, w, tm=256):
    m, k = x.shape
    n = w.shape[1]
    n_pad = -(-n // 128) * 128
    wb = w.astype(jnp.bfloat16)
    if n_pad != n:
        wb = jnp.pad(wb, ((0, 0), (0, n_pad - n)))
    tm = min(tm, m)
    out = pl.pallas_call(
        _mm_body,
        grid=(m // tm,),
        in_specs=[pl.BlockSpec((tm, k), lambda i: (i, 0)),
                  pl.BlockSpec((k, n_pad), lambda i: (0, 0))],
        out_specs=pl.BlockSpec((tm, n_pad), lambda i: (i, 0)),
        out_shape=jax.ShapeDtypeStruct((m, n_pad), jnp.float32),
        compiler_params=pltpu.CompilerParams(dimension_semantics=("parallel",),
                                             vmem_limit_bytes=V7X_VMEM_LIMIT_BYTES),
        name="dense_matmul",
    )(x, wb)
    return out[:, :n] if n_pad != n else out


def _mm3(x, w):
    b, t, k = x.shape
    return _matmul(x.reshape(b * t, k), w).reshape(b, t, w.shape[1])


def _layernorm(x, g, b):
    mu = jnp.mean(x, -1, keepdims=True)
    var = jnp.mean(jnp.square(x - mu), -1, keepdims=True)
    return (x - mu) * lax.rsqrt(var + LN_EPS) * g + b


def _rmsnorm(x, w, eps):
    return x * lax.rsqrt(jnp.mean(x * x, -1, keepdims=True) + eps) * w


def _l2norm(x):
    return x * lax.rsqrt(jnp.sum(x * x, -1, keepdims=True) + 1e-6)


def _rope(x, pos):
    half = ROT_DIM // 2
    inv = ROPE_THETA ** (-jnp.arange(0, ROT_DIM, 2, dtype=jnp.float32) / ROT_DIM)
    ang = pos.astype(jnp.float32)[:, None] * inv
    cos = jnp.cos(ang)[:, None, None, :]
    sin = jnp.sin(ang)[:, None, None, :]
    xr = x[..., :ROT_DIM]
    x1, x2 = xr[..., :half], xr[..., half:]
    rot = jnp.concatenate([x1 * cos - x2 * sin, x2 * cos + x1 * sin], -1)
    return jnp.concatenate([rot, x[..., ROT_DIM:]], -1)


def _diff_attn_prompt(q, k, v, lam):
    B, S = q.shape[:2]
    scale = DA_DH ** -0.5
    kpos = jnp.arange(S)

    def block(i):
        q0 = i * Q_BLOCK
        qb = lax.dynamic_slice_in_dim(q, q0, Q_BLOCK, axis=1)
        s = jnp.einsum('bqhcd,bkhcd->bhcqk', qb, k, preferred_element_type=jnp.float32) * scale
        qpos = q0 + jnp.arange(Q_BLOCK)
        mask = kpos[None, :] <= qpos[:, None]
        p = jax.nn.softmax(jnp.where(mask, s, NEG), axis=-1)
        a = p[:, :, 0] - lam * p[:, :, 1]
        return jnp.einsum('bhqk,bkhd->bqhd', a, v)

    o = lax.map(block, jnp.arange(S // Q_BLOCK))
    return jnp.moveaxis(o, 0, 1).reshape(B, S, DA_HEADS, DA_DV)


def _softmax_update(carry, s, vals):
    m, l, acc = carry
    m_new = jnp.maximum(m, jnp.max(s, -1))
    corr = jnp.exp(m - m_new)
    p = jnp.exp(s - m_new[..., None])
    l = l * corr + jnp.sum(p, -1)
    acc = acc * corr[..., None] + jnp.einsum('bhctk,bkhd->bhctd', p, vals)
    return (m_new, l, acc)


def _diff_attn_sample(q, k, v, lam, cache_k, cache_v, page_table):
    B, T = q.shape[:2]
    page = cache_k.shape[1]
    qf = q * DA_DH ** -0.5

    def page_step(carry, ids):
        kp = cache_k[ids].reshape(B, page, DA_HEADS, 2, DA_DH)
        vp = cache_v[ids]
        s = jnp.einsum('bthcd,bkhcd->bhctk', qf, kp)
        return _softmax_update(carry, s, vp), None

    init = (jnp.full((B, DA_HEADS, 2, T), NEG, jnp.float32),
            jnp.zeros((B, DA_HEADS, 2, T), jnp.float32),
            jnp.zeros((B, DA_HEADS, 2, T, DA_DV), jnp.float32))
    carry, _ = lax.scan(page_step, init, page_table.T)
    s = jnp.einsum('bthcd,bkhcd->bhctk', qf, k)
    mask = jnp.arange(T)[None, :] <= jnp.arange(T)[:, None]
    m, l, acc = _softmax_update(carry, jnp.where(mask, s, NEG), v)
    o = acc / l[..., None]
    o = o[:, :, 0] - lam * o[:, :, 1]
    return jnp.transpose(o, (0, 2, 1, 3))


def _gated_delta(q, k, v, beta, g, state0):
    B, T, H, DK = q.shape
    C = DN_CHUNK if T % DN_CHUNK == 0 else T
    NC = T // C

    def chunks(a):
        a = a.reshape((B, NC, C, H) + a.shape[3:])
        return jnp.moveaxis(a, 3, 2)

    q, k, v, beta, g = chunks(q), chunks(k), chunks(v), chunks(beta), chunks(g)
    G = jnp.cumsum(g, -1)
    incl = jnp.tril(jnp.ones((C, C), bool))
    strict = jnp.tril(jnp.ones((C, C), bool), -1)
    D = jnp.exp(jnp.where(incl, G[..., :, None] - G[..., None, :], -jnp.inf))
    kk = jnp.einsum('...id,...jd->...ij', k, k)
    L = jnp.where(strict, beta[..., :, None] * kk * D, 0.0) + jnp.eye(C, dtype=jnp.float32)
    W = lax.linalg.triangular_solve(L, beta[..., None] * v, left_side=True, lower=True, unit_diagonal=True)
    Y = lax.linalg.triangular_solve(L, (beta * jnp.exp(G))[..., None] * k, left_side=True, lower=True,
                                    unit_diagonal=True)
    P = jnp.einsum('...id,...jd->...ij', q, k) * D
    qg = q * jnp.exp(G)[..., None]
    kg = k * jnp.exp(G[..., -1:] - G)[..., None]
    glast = jnp.exp(G[..., -1])

    def step(S, xs):
        Wc, Yc, Pc, qgc, kgc, glc = xs
        U = Wc - Yc @ S
        O = qgc @ S + Pc @ U
        S = glc[..., None, None] * S + jnp.swapaxes(kgc, -1, -2) @ U
        return S, O

    xs = tuple(jnp.moveaxis(a, 1, 0) for a in (W, Y, P, qg, kg, glast))
    S, O = lax.scan(step, state0, xs)
    O = jnp.transpose(O, (1, 0, 3, 2, 4)).reshape(B, T, H, DK)
    return O, S


def _token_mixer(x, pos, attend, conv0, state0, lw, lam_init):
    B, T, _ = x.shape
    cuts = [int(c) for c in np.cumsum(IN_SPLITS)[:-1]]
    da_q, da_k, da_v, dn_qkv, dn_a, dn_b, dn_z = jnp.split(_mm3(x, lw['w_in']), cuts, axis=-1)
    q = _rope(da_q.reshape(B, T, DA_HEADS, 2, DA_DH), pos)
    k = _rope(da_k.reshape(B, T, DA_HEADS, 2, DA_DH), pos)
    v = da_v.reshape(B, T, DA_HEADS, DA_DV)
    lam = (jnp.exp(jnp.sum(lw['da_lambda_q1'] * lw['da_lambda_k1']))
           - jnp.exp(jnp.sum(lw['da_lambda_q2'] * lw['da_lambda_k2'])) + lam_init)
    o_da = attend(q, k, v, lam)
    o_da = (_rmsnorm(o_da, lw['da_norm_w'], LN_EPS) * (1.0 - lam_init)).reshape(B, T, DA_WIDTH)
    xp = jnp.concatenate([conv0, dn_qkv], axis=1)
    cw = lw['dn_conv_w']
    conv = xp[:, 0:T] * cw[0]
    for i in range(1, CONV_W):
        conv = conv + xp[:, i:i + T] * cw[i]
    conv = jax.nn.silu(conv)
    new_conv = xp[:, T:]
    dq, dk, dv = jnp.split(conv, 3, axis=-1)
    dq = _l2norm(dq.reshape(B, T, DN_HEADS, DN_DH)) * DN_DH ** -0.5
    dk = _l2norm(dk.reshape(B, T, DN_HEADS, DN_DH))
    dv = dv.reshape(B, T, DN_HEADS, DN_DH)
    beta = jax.nn.sigmoid(dn_b)
    g = -jnp.exp(lw['dn_a_log']) * jax.nn.softplus(dn_a + lw['dn_dt_bias'])
    o_dn, new_state = _gated_delta(dq, dk, dv, beta, g, state0)
    z = dn_z.reshape(B, T, DN_HEADS, DN_DH)
    o_dn = (_rmsnorm(o_dn, lw['dn_norm_w'], 1e-6) * jax.nn.silu(z)).reshape(B, T, DN_WIDTH)
    out = _mm3(jnp.concatenate([o_da, o_dn], -1), lw['w_out'])
    return out, k.reshape(B, T, DA_HEADS, DA_QK), v, new_state, new_conv


def _mem_kv(mem, w_ckv):
    B, M, _ = mem.shape
    mk, mv = jnp.split(_mm3(mem, w_ckv), 2, axis=-1)
    return mk.reshape(B, M, MEM_HEADS, MEM_DH), mv.reshape(B, M, MEM_HEADS, MEM_DH)


def _cross_attn(x, mem_k, mem_v, w_cq, w_co):
    B, T, _ = x.shape
    q = _mm3(x, w_cq).reshape(B, T, MEM_HEADS, MEM_DH)
    s = jnp.einsum('bthd,bmhd->bhtm', q, mem_k, preferred_element_type=jnp.float32) * MEM_DH ** -0.5
    p = jax.nn.softmax(s, axis=-1)
    o = jnp.einsum('bhtm,bmhd->bthd', p, mem_v).reshape(B, T, MEM_WIDTH)
    return _mm3(o, w_co)


def _routed_experts(xf, idx, gate, w_gu, w_down):
    n_tok, d = xf.shape
    n_assign = n_tok * TOP_K
    flat_e = idx.reshape(-1)
    flat_g = gate.reshape(-1)
    order = jnp.argsort(flat_e)
    sorted_e = flat_e[order]
    counts = jnp.bincount(flat_e, length=N_EXPERTS)
    padded = (counts + EXPERT_BLOCK - 1) // EXPERT_BLOCK * EXPERT_BLOCK
    pad_end = jnp.cumsum(padded)
    pad_start = pad_end - padded
    start = jnp.cumsum(counts) - counts
    dest = pad_start[sorted_e] + jnp.arange(n_assign) - start[sorted_e]
    n_blocks = -(-(n_assign + N_EXPERTS * (EXPERT_BLOCK - 1)) // EXPERT_BLOCK)
    n_rows = n_blocks * EXPERT_BLOCK
    row_tok = jnp.full((n_rows,), n_tok, jnp.int32).at[dest].set((order // TOP_K).astype(jnp.int32))
    row_gate = jnp.zeros((n_rows,), xf.dtype).at[dest].set(flat_g[order].astype(xf.dtype))
    block_exp = jnp.minimum(jnp.searchsorted(pad_end, jnp.arange(n_blocks) * EXPERT_BLOCK, side='right'),
                            N_EXPERTS - 1)
    x_pad = jnp.concatenate([xf, jnp.zeros((1, d), xf.dtype)], 0)

    def body(acc, xs):
        tok, rg, e = xs
        xb = x_pad[tok]
        gg, uu = jnp.split(xb @ w_gu[e], 2, axis=-1)
        yb = (jax.nn.silu(gg) * uu) @ w_down[e]
        return acc.at[tok].add(yb * rg[:, None]), None

    acc, _ = lax.scan(body, jnp.zeros((n_tok + 1, d), xf.dtype),
                      (row_tok.reshape(n_blocks, EXPERT_BLOCK), row_gate.reshape(n_blocks, EXPERT_BLOCK), block_exp))
    return acc[:n_tok]


def _moe(x, w_router, router_bias, w_exp_gu, w_exp_down, w_sh_gu, w_sh_down):
    B, T, D = x.shape
    xf = x.reshape(B * T, D)
    n_tok = B * T
    scores = jax.nn.sigmoid(jnp.dot(xf, w_router, precision=lax.Precision.HIGHEST))
    biased = scores + router_bias
    grp = biased.reshape(n_tok, N_GROUPS, N_EXPERTS // N_GROUPS)
    grp_score = jnp.sum(lax.top_k(grp, 2)[0], -1)
    _, top_g = lax.top_k(grp_score, TOPK_GROUPS)
    gmask = jnp.any(top_g[..., None] == jnp.arange(N_GROUPS), axis=-2)
    emask = jnp.repeat(gmask, N_EXPERTS // N_GROUPS, axis=-1)
    _, idx = lax.top_k(jnp.where(emask, biased, -jnp.inf), TOP_K)
    gate = jnp.take_along_axis(scores, idx, axis=-1)
    gate = gate / jnp.sum(gate, -1, keepdims=True) * ROUTED_SCALE
    routed = _routed_experts(xf, idx, gate, w_exp_gu, w_exp_down)
    gs, us = jnp.split(_matmul(xf, w_sh_gu), 2, axis=-1)
    shared = _matmul(jax.nn.silu(gs) * us, w_sh_down)
    return (routed + shared).reshape(B, T, D)


def _layer(x, pos, attend, conv0, state0, mem_k, mem_v, lw, lam_init, alpha):
    h, k_rows, v_rows, state, conv = _token_mixer(x, pos, attend, conv0, state0, lw, lam_init)
    x = _layernorm(alpha * x + h, lw['ln1_g'], lw['ln1_b'])
    x = _layernorm(alpha * x + _cross_attn(x, mem_k, mem_v, lw['w_cq'], lw['w_co']),
                   lw['ln2_g'], lw['ln2_b'])
    x = _layernorm(alpha * x + _moe(x, lw['w_router'], lw['router_bias'], lw['w_exp_gu'],
                                    lw['w_exp_down'], lw['w_sh_gu'], lw['w_sh_down']),
                   lw['ln3_g'], lw['ln3_b'])
    return x, k_rows, v_rows, state, conv


def kernel(x_prompt, x_sample, cache_attn_k, cache_attn_v, cache_mem_k, cache_mem_v, state_dn, state_dn_conv,
           page_table, mem_prompt, w_in, da_lambda_q1, da_lambda_k1, da_lambda_q2, da_lambda_k2, da_norm_w,
           dn_conv_w, dn_a_log, dn_dt_bias, dn_norm_w, w_out, ln1_g, ln1_b, w_cq, w_ckv, w_co, ln2_g, ln2_b,
           w_router, router_bias, w_exp_gu, w_exp_down, w_sh_gu, w_sh_down, ln3_g, ln3_b):
    depth = w_in.shape[0]
    alpha = (2 * depth) ** 0.25
    Bp, Sp, _ = x_prompt.shape
    Bs, Ts, _ = x_sample.shape
    past = page_table.shape[1] * cache_attn_k.shape[2]
    pos_p = jnp.arange(Sp, dtype=jnp.int32)
    pos_s = past + jnp.arange(Ts, dtype=jnp.int32)
    yp, ys = x_prompt, x_sample
    outs = [[] for _ in range(10)]
    for l in range(depth):
        lw = {'w_in': w_in[l], 'da_lambda_q1': da_lambda_q1[l], 'da_lambda_k1': da_lambda_k1[l],
              'da_lambda_q2': da_lambda_q2[l], 'da_lambda_k2': da_lambda_k2[l], 'da_norm_w': da_norm_w[l],
              'dn_conv_w': dn_conv_w[l], 'dn_a_log': dn_a_log[l], 'dn_dt_bias': dn_dt_bias[l],
              'dn_norm_w': dn_norm_w[l], 'w_out': w_out[l], 'ln1_g': ln1_g[l], 'ln1_b': ln1_b[l],
              'w_cq': w_cq[l], 'w_co': w_co[l], 'ln2_g': ln2_g[l], 'ln2_b': ln2_b[l],
              'w_router': w_router[l], 'router_bias': router_bias[l], 'w_exp_gu': w_exp_gu[l],
              'w_exp_down': w_exp_down[l], 'w_sh_gu': w_sh_gu[l], 'w_sh_down': w_sh_down[l],
              'ln3_g': ln3_g[l], 'ln3_b': ln3_b[l]}
        lam_init = 0.8 - 0.6 * math.exp(-0.3 * l)
        mk_p, mv_p = _mem_kv(mem_prompt, w_ckv[l])
        conv0 = jnp.zeros((Bp, CONV_W - 1, 3 * DN_WIDTH), x_prompt.dtype)
        st0 = jnp.zeros((Bp, DN_HEADS, DN_DH, DN_DH), x_prompt.dtype)
        yp, kp, vp, sp, cp = _layer(yp, pos_p, _diff_attn_prompt, conv0, st0, mk_p, mv_p, lw, lam_init, alpha)
        attend_s = functools.partial(_diff_attn_sample, cache_k=cache_attn_k[l], cache_v=cache_attn_v[l],
                                     page_table=page_table)
        ys, ks_, vs_, ss_, cs_ = _layer(ys, pos_s, attend_s, state_dn_conv[l], state_dn[l], cache_mem_k[l],
                                        cache_mem_v[l], lw, lam_init, alpha)
        for lst, val in zip(outs, (kp, vp, ks_, vs_, sp, ss_, cp, cs_, mk_p, mv_p)):
            lst.append(val)
    return (yp, ys) + tuple(jnp.stack(o) for o in outs)
```

```python
import functools
import math

import jax
import jax.numpy as jnp
import numpy as np
from jax import lax
from jax.experimental import pallas as pl
from jax.experimental.pallas import tpu as pltpu

D_MODEL = 1024
DA_HEADS = 4
DA_DH = 64
DA_QK = 2 * DA_DH
DA_DV = 2 * DA_DH
DA_WIDTH = DA_HEADS * DA_DV
ROT_DIM = DA_DH // 4
ROPE_THETA = 500000.0
DN_HEADS = 4
DN_DH = 128
DN_WIDTH = DN_HEADS * DN_DH
CONV_W = 4
DN_CHUNK = 64
MEM_HEADS = 4
MEM_DH = 128
MEM_WIDTH = MEM_HEADS * MEM_DH
N_EXPERTS = 256
TOP_K = 8
N_GROUPS = 8
TOPK_GROUPS = 4
D_EXPERT = 256
ROUTED_SCALE = 2.5
LN_EPS = 1e-5
NEG = -1e30

LANES = 128
V7X_VMEM_LIMIT_BYTES = 48 * 1024 * 1024
ROW_TILE = 512
ATTN_TILE = 512
MOE_ROWS = 256
IN_MAIN = 3 * DA_WIDTH + 3 * DN_WIDTH
IN_AB = 2 * DN_HEADS
IN_PERM_COLS = IN_MAIN + DN_WIDTH + LANES

BF16 = jnp.bfloat16
F32 = jnp.float32


def _params(n_axes=1):
    return pltpu.CompilerParams(dimension_semantics=("parallel",) * n_axes,
                                vmem_limit_bytes=V7X_VMEM_LIMIT_BYTES)


def _row_tile(n):
    return ROW_TILE if n % ROW_TILE == 0 else n


def _in_proj_body(x_ref, w_ref, c_ref, sa_ref, sb_ref,
                  q_ref, kf_ref, vf_ref, kb_ref, vb_ref, dn_ref, z_ref, ab_ref):
    h = jnp.dot(x_ref[...].astype(BF16), w_ref[...], preferred_element_type=F32)
    reps = DA_WIDTH // LANES
    c = jnp.concatenate([c_ref[...]] * reps, axis=1)
    sa = jnp.concatenate([sa_ref[...]] * reps, axis=1)
    sb = jnp.concatenate([sb_ref[...]] * reps, axis=1)

    def rope(t):
        return (t * c + pltpu.roll(t, DA_WIDTH - ROT_DIM // 2, 1) * sa
                + pltpu.roll(t, ROT_DIM // 2, 1) * sb)

    q = rope(h[:, 0:DA_WIDTH])
    k = rope(h[:, DA_WIDTH:2 * DA_WIDTH])
    v = h[:, 2 * DA_WIDTH:3 * DA_WIDTH]
    q_ref[...] = (q * DA_DH ** -0.5).astype(BF16)
    kf_ref[...] = k
    vf_ref[...] = v
    kb_ref[...] = k.astype(BF16)
    vb_ref[...] = v.astype(BF16)
    dn_ref[...] = h[:, 3 * DA_WIDTH:IN_MAIN]
    z_ref[...] = h[:, IN_MAIN:IN_MAIN + DN_WIDTH]
    ab_ref[...] = h[:, IN_MAIN + DN_WIDTH:]


def _in_proj(x, w_perm, tables, n_pos_blocks, tm):
    n = x.shape[0]
    row = lambda w: pl.BlockSpec((tm, w), lambda i: (i, 0))
    tab = pl.BlockSpec((tm, LANES), lambda i: (i % n_pos_blocks, 0))
    widths = (DA_WIDTH, DA_WIDTH, DA_WIDTH, DA_WIDTH, DA_WIDTH, 3 * DN_WIDTH, DN_WIDTH, LANES)
    dtypes = (BF16, F32, F32, BF16, BF16, F32, F32, F32)
    return pl.pallas_call(
        _in_proj_body,
        grid=(n // tm,),
        in_specs=[row(D_MODEL), pl.BlockSpec((D_MODEL, IN_PERM_COLS), lambda i: (0, 0)), tab, tab, tab],
        out_specs=[row(w) for w in widths],
        out_shape=[jax.ShapeDtypeStruct((n, w), d) for w, d in zip(widths, dtypes)],
        compiler_params=_params(),
        name="in_proj_rope",
    )(x, w_perm, *tables)


def _rope_tables(pos):
    half = ROT_DIM // 2
    inv = ROPE_THETA ** (-jnp.arange(0, ROT_DIM, 2, dtype=F32) / ROT_DIM)
    ang = pos.astype(F32)[:, None] * inv
    cos, sin = jnp.cos(ang), jnp.sin(ang)
    n = pos.shape[0]
    ones = jnp.ones((n, DA_DH - ROT_DIM), F32)
    zeros = jnp.zeros((n, DA_DH - ROT_DIM), F32)
    zh = jnp.zeros((n, half), F32)
    c = jnp.concatenate([cos, cos, ones], 1)
    sa = jnp.concatenate([-sin, zh, zeros], 1)
    sb = jnp.concatenate([zh, sin, zeros], 1)
    return tuple(jnp.concatenate([t, t], 1) for t in (c, sa, sb))


def _flash_body(lam_ref, q_ref, k_ref, v_ref, nw_ref, o_ref, *, t, out_scale):
    qi = pl.program_id(2)
    q = q_ref[0]
    lane = lax.broadcasted_iota(jnp.int32, q.shape, 1)
    zero = jnp.zeros_like(q)
    qq = jnp.concatenate([jnp.where(lane < DA_DH, q, zero), jnp.where(lane >= DA_DH, q, zero)], axis=0)

    def update(j, carry, masked):
        m, l, acc = carry
        start = pl.multiple_of(j * t, t)
        kb = k_ref[0, pl.ds(start, t), :]
        vb = v_ref[0, pl.ds(start, t), :]
        s = lax.dot_general(qq, kb, (((1,), (1,)), ((), ())), preferred_element_type=F32)
        if masked:
            r = lax.broadcasted_iota(jnp.int32, s.shape, 0)
            c = lax.broadcasted_iota(jnp.int32, s.shape, 1)
            r = jnp.where(r >= t, r - t, r)
            s = jnp.where(c <= r, s, NEG)
        m_new = jnp.maximum(m, jnp.max(s, axis=-1, keepdims=True))
        corr = jnp.exp(m - m_new)
        p = jnp.exp(s - m_new)
        l = l * corr + jnp.sum(p, axis=-1, keepdims=True)
        acc = acc * corr + jnp.dot(p.astype(BF16), vb, preferred_element_type=F32)
        return m_new, l, acc

    init = (jnp.full((2 * t, 1), NEG, F32), jnp.zeros((2 * t, 1), F32), jnp.zeros((2 * t, DA_DV), F32))
    carry = lax.fori_loop(0, qi, lambda j, c: update(j, c, False), init)
    m, l, acc = update(qi, carry, True)
    o = acc / l
    o = o[:t] - lam_ref[0] * o[t:]
    o = o * lax.rsqrt(jnp.mean(o * o, axis=-1, keepdims=True) + LN_EPS) * nw_ref[...] * out_scale
    o_ref[0] = o.astype(o_ref.dtype)


def _diff_attn_prompt(qb, kb, vb, lam, norm_w, out_scale):
    b, s, _ = qb.shape
    t = min(ATTN_TILE, s)
    return pl.pallas_call(
        functools.partial(_flash_body, t=t, out_scale=out_scale),
        grid=(b, DA_HEADS, s // t),
        in_specs=[pl.BlockSpec(memory_space=pltpu.SMEM),
                  pl.BlockSpec((1, t, DA_QK), lambda bi, h, i: (bi, i, h)),
                  pl.BlockSpec((1, s, DA_QK), lambda bi, h, i: (bi, 0, h)),
                  pl.BlockSpec((1, s, DA_DV), lambda bi, h, i: (bi, 0, h)),
                  pl.BlockSpec((1, DA_DV), lambda bi, h, i: (0, 0))],
        out_specs=pl.BlockSpec((1, t, DA_DV), lambda bi, h, i: (bi, i, h)),
        out_shape=jax.ShapeDtypeStruct((b, s, DA_WIDTH), BF16),
        compiler_params=_params(3),
        name="diff_attn_prompt",
    )(lam.reshape(1), qb, kb, vb, norm_w.reshape(1, DA_DV))


def _mm_body(x_ref, w_ref, o_ref):
    o_ref[...] = jnp.dot(x_ref[...].astype(BF16), w_ref[...], preferred_element_type=F32).astype(o_ref.dtype)


def _matmul(x, w, out_dtype=F32):
    m, k = x.shape
    n = w.shape[1]
    tm = _row_tile(m)
    return pl.pallas_call(
        _mm_body,
        grid=(m // tm,),
        in_specs=[pl.BlockSpec((tm, k), lambda i: (i, 0)), pl.BlockSpec((k, n), lambda i: (0, 0))],
        out_specs=pl.BlockSpec((tm, n), lambda i: (i, 0)),
        out_shape=jax.ShapeDtypeStruct((m, n), out_dtype),
        compiler_params=_params(),
        name="dense_matmul",
    )(x, w)


def _ln(x, g, b):
    mu = jnp.mean(x, axis=-1, keepdims=True)
    d = x - mu
    var = jnp.mean(d * d, axis=-1, keepdims=True)
    return d * lax.rsqrt(var + LN_EPS) * g + b


def _mm_res_ln_body(*refs, n_x, alpha, with_logits):
    xs = refs[:n_x]
    ws = refs[n_x:2 * n_x]
    res_ref, g_ref, b_ref = refs[2 * n_x:2 * n_x + 3]
    rest = refs[2 * n_x + 3:]
    h = alpha * res_ref[...]
    for x_ref, w_ref in zip(xs, ws):
        h = h + jnp.dot(x_ref[...].astype(BF16), w_ref[...], preferred_element_type=F32)
    y = _ln(h, g_ref[...], b_ref[...])
    if with_logits:
        wr_ref, y_ref, yb_ref, lg_ref = rest
    else:
        y_ref, yb_ref = rest
    yb = y.astype(BF16)
    y_ref[...] = y
    yb_ref[...] = yb
    if with_logits:
        lg_ref[...] = jnp.dot(yb, wr_ref[...], preferred_element_type=F32)


def _mm_res_ln(xs, ws, res, g, b, alpha, w_router=None):
    m, d = res.shape
    tm = _row_tile(m)
    row = lambda w: pl.BlockSpec((tm, w), lambda i: (i, 0))
    full = lambda a: pl.BlockSpec(a.shape, lambda i: (0, 0))
    g2, b2 = g.reshape(1, d), b.reshape(1, d)
    in_specs = [row(x.shape[1]) for x in xs] + [full(w) for w in ws] + [row(d), full(g2), full(b2)]
    args = list(xs) + list(ws) + [res, g2, b2]
    out_specs = [row(d), row(d)]
    out_shape = [jax.ShapeDtypeStruct((m, d), F32), jax.ShapeDtypeStruct((m, d), BF16)]
    if w_router is not None:
        in_specs.append(full(w_router))
        args.append(w_router)
        out_specs.append(row(w_router.shape[1]))
        out_shape.append(jax.ShapeDtypeStruct((m, w_router.shape[1]), F32))
    return pl.pallas_call(
        functools.partial(_mm_res_ln_body, n_x=len(xs), alpha=alpha, with_logits=w_router is not None),
        grid=(m // tm,),
        in_specs=in_specs, out_specs=out_specs, out_shape=out_shape,
        compiler_params=_params(),
        name="matmul_residual_layernorm",
    )(*args)


def _silu(x):
    return x * jax.nn.sigmoid(x)


def _moe_out_body(x_ref, r_ref, wgu_ref, wd_ref, g_ref, b_ref, y_ref, *, alpha):
    x = x_ref[...]
    h = jnp.dot(x.astype(BF16), wgu_ref[...], preferred_element_type=F32)
    d_sh = h.shape[1] // 2
    a = (_silu(h[:, :d_sh]) * h[:, d_sh:]).astype(BF16)
    shared = jnp.dot(a, wd_ref[...], preferred_element_type=F32)
    y_ref[...] = _ln(alpha * x + (r_ref[...] + shared), g_ref[...], b_ref[...])


def _moe_out(x, routed, w_sh_gu, w_sh_down, g, b, alpha):
    m, d = x.shape
    tm = _row_tile(m)
    row = pl.BlockSpec((tm, d), lambda i: (i, 0))
    full = lambda a: pl.BlockSpec(a.shape, lambda i: (0, 0))
    g2, b2 = g.reshape(1, d), b.reshape(1, d)
    return pl.pallas_call(
        functools.partial(_moe_out_body, alpha=alpha),
        grid=(m // tm,),
        in_specs=[row, row, full(w_sh_gu), full(w_sh_down), full(g2), full(b2)],
        out_specs=row,
        out_shape=jax.ShapeDtypeStruct((m, d), F32),
        compiler_params=_params(),
        name="shared_expert_layernorm",
    )(x, routed, w_sh_gu, w_sh_down, g2, b2)


def _experts_body(bexp_ref, x_ref, wgu_ref, wd_ref, o_ref):
    del bexp_ref
    h = jnp.dot(x_ref[...], wgu_ref[0], preferred_element_type=F32)
    a = (_silu(h[:, :D_EXPERT]) * h[:, D_EXPERT:]).astype(BF16)
    o_ref[...] = jnp.dot(a, wd_ref[0], preferred_element_type=F32).astype(o_ref.dtype)


def _experts(x_sorted, block_exp, w_gu, w_down):
    n_rows, d = x_sorted.shape
    n_blocks = n_rows // MOE_ROWS
    grid_spec = pltpu.PrefetchScalarGridSpec(
        num_scalar_prefetch=1,
        grid=(n_blocks,),
        in_specs=[pl.BlockSpec((MOE_ROWS, d), lambda i, be: (i, 0)),
                  pl.BlockSpec((1, d, 2 * D_EXPERT), lambda i, be: (be[i], 0, 0)),
                  pl.BlockSpec((1, D_EXPERT, d), lambda i, be: (be[i], 0, 0))],
        out_specs=pl.BlockSpec((MOE_ROWS, d), lambda i, be: (i, 0)),
    )
    return pl.pallas_call(
        _experts_body,
        grid_spec=grid_spec,
        out_shape=jax.ShapeDtypeStruct((n_rows, d), BF16),
        compiler_params=pltpu.CompilerParams(dimension_semantics=("arbitrary",),
                                             vmem_limit_bytes=V7X_VMEM_LIMIT_BYTES),
        name="routed_experts",
    )(block_exp, x_sorted, w_gu, w_down)


def _route(logits, router_bias):
    n_tok = logits.shape[0]
    scores = jax.nn.sigmoid(logits)
    biased = scores + router_bias
    grp = biased.reshape(n_tok, N_GROUPS, N_EXPERTS // N_GROUPS)
    grp_score = jnp.sum(lax.top_k(grp, 2)[0], -1)
    _, top_g = lax.top_k(grp_score, TOPK_GROUPS)
    gmask = jnp.any(top_g[..., None] == jnp.arange(N_GROUPS), axis=-2)
    emask = jnp.repeat(gmask, N_EXPERTS // N_GROUPS, axis=-1)
    _, idx = lax.top_k(jnp.where(emask, biased, -jnp.inf), TOP_K)
    gate = jnp.take_along_axis(scores, idx, axis=-1)
    gate = gate / jnp.sum(gate, -1, keepdims=True) * ROUTED_SCALE
    return idx, gate


def _routed(xb_parts, idx, gate, w_gu, w_down):
    n_tok = idx.shape[0]
    n_assign = n_tok * TOP_K
    flat_e = idx.reshape(-1)
    order = jnp.argsort(flat_e)
    sorted_e = flat_e[order]
    counts = jnp.bincount(flat_e, length=N_EXPERTS)
    padded = (counts + MOE_ROWS - 1) // MOE_ROWS * MOE_ROWS
    pad_end = jnp.cumsum(padded)
    pad_start = pad_end - padded
    start = jnp.cumsum(counts) - counts
    dest = (pad_start[sorted_e] + jnp.arange(n_assign) - start[sorted_e]).astype(jnp.int32)
    n_blocks = -(-(n_assign + N_EXPERTS * (MOE_ROWS - 1)) // MOE_ROWS)
    n_rows = n_blocks * MOE_ROWS
    row_tok = jnp.full((n_rows,), n_tok, jnp.int32).at[dest].set((order // TOP_K).astype(jnp.int32))
    slot_row = jnp.zeros((n_assign,), jnp.int32).at[order].set(dest)
    block_exp = jnp.minimum(jnp.searchsorted(pad_end, jnp.arange(n_blocks) * MOE_ROWS, side='right'),
                            N_EXPERTS - 1).astype(jnp.int32)
    d = xb_parts[0].shape[1]
    x_pad = jnp.concatenate(list(xb_parts) + [jnp.zeros((1, d), BF16)], 0)
    y_sorted = _experts(x_pad[row_tok], block_exp, w_gu, w_down)
    outs, t0 = [], 0
    for part in xb_parts:
        t1 = t0 + part.shape[0]
        rows = slot_row[t0 * TOP_K:t1 * TOP_K]
        y = y_sorted[rows].astype(F32).reshape(t1 - t0, TOP_K, d)
        outs.append(jnp.sum(y * gate[t0:t1, :, None], axis=1))
        t0 = t1
    return outs


def _l2norm(x):
    return x * lax.rsqrt(jnp.sum(x * x, -1, keepdims=True) + 1e-6)


def _softmax_update(carry, s, vals):
    m, l, acc = carry
    m_new = jnp.maximum(m, jnp.max(s, -1))
    corr = jnp.exp(m - m_new)
    p = jnp.exp(s - m_new[..., None])
    l = l * corr + jnp.sum(p, -1)
    acc = acc * corr[..., None] + jnp.einsum('bhctk,bkhd->bhctd', p, vals)
    return (m_new, l, acc)


def _diff_attn_sample(qf, k, v, lam, cache_k, cache_v, page_table):
    B, T = qf.shape[:2]
    page = cache_k.shape[1]

    def page_step(carry, ids):
        kp = cache_k[ids].reshape(B, page, DA_HEADS, 2, DA_DH)
        vp = cache_v[ids]
        s = jnp.einsum('bthcd,bkhcd->bhctk', qf, kp)
        return _softmax_update(carry, s, vp), None

    init = (jnp.full((B, DA_HEADS, 2, T), NEG, F32),
            jnp.zeros((B, DA_HEADS, 2, T), F32),
            jnp.zeros((B, DA_HEADS, 2, T, DA_DV), F32))
    carry, _ = lax.scan(page_step, init, page_table.T)
    s = jnp.einsum('bthcd,bkhcd->bhctk', qf, k)
    mask = jnp.arange(T)[None, :] <= jnp.arange(T)[:, None]
    m, l, acc = _softmax_update(carry, jnp.where(mask, s, NEG), v)
    o = acc / l[..., None]
    o = o[:, :, 0] - lam * o[:, :, 1]
    return jnp.transpose(o, (0, 2, 1, 3))


def _gated_delta(q, k, v, beta, g, state0):
    B, T, H, DK = q.shape
    C = DN_CHUNK if T % DN_CHUNK == 0 else T
    NC = T // C

    def chunks(a):
        a = a.reshape((B, NC, C, H) + a.shape[3:])
        return jnp.moveaxis(a, 3, 2)

    q, k, v, beta, g = chunks(q), chunks(k), chunks(v), chunks(beta), chunks(g)
    G = jnp.cumsum(g, -1)
    incl = jnp.tril(jnp.ones((C, C), bool))
    strict = jnp.tril(jnp.ones((C, C), bool), -1)
    D = jnp.exp(jnp.where(incl, G[..., :, None] - G[..., None, :], -jnp.inf))
    kk = jnp.einsum('...id,...jd->...ij', k, k)
    L = jnp.where(strict, beta[..., :, None] * kk * D, 0.0) + jnp.eye(C, dtype=F32)
    W = lax.linalg.triangular_solve(L, beta[..., None] * v, left_side=True, lower=True, unit_diagonal=True)
    Y = lax.linalg.triangular_solve(L, (beta * jnp.exp(G))[..., None] * k, left_side=True, lower=True,
                                    unit_diagonal=True)
    P = jnp.einsum('...id,...jd->...ij', q, k) * D
    qg = q * jnp.exp(G)[..., None]
    kg = k * jnp.exp(G[..., -1:] - G)[..., None]
    glast = jnp.exp(G[..., -1])

    def step(S, xs):
        Wc, Yc, Pc, qgc, kgc, glc = xs
        U = Wc - Yc @ S
        O = qgc @ S + Pc @ U
        S = glc[..., None, None] * S + jnp.swapaxes(kgc, -1, -2) @ U
        return S, O

    xs = tuple(jnp.moveaxis(a, 1, 0) for a in (W, Y, P, qg, kg, glast))
    S, O = lax.scan(step, state0, xs)
    O = jnp.transpose(O, (1, 0, 3, 2, 4)).reshape(B, T, H, DK)
    return O, S


def _delta_group(dn_qkv, ab, z, conv0, state0, lw):
    B, T, _ = dn_qkv.shape
    xp = jnp.concatenate([conv0, dn_qkv], axis=1)
    cw = lw['dn_conv_w']
    conv = xp[:, 0:T] * cw[0]
    for i in range(1, CONV_W):
        conv = conv + xp[:, i:i + T] * cw[i]
    conv = _silu(conv)
    new_conv = xp[:, T:]
    dq, dk, dv = jnp.split(conv, 3, axis=-1)
    dq = _l2norm(dq.reshape(B, T, DN_HEADS, DN_DH)) * DN_DH ** -0.5
    dk = _l2norm(dk.reshape(B, T, DN_HEADS, DN_DH))
    dv = dv.reshape(B, T, DN_HEADS, DN_DH)
    dn_a, dn_b = ab[..., :DN_HEADS], ab[..., DN_HEADS:IN_AB]
    beta = jax.nn.sigmoid(dn_b)
    g = -jnp.exp(lw['dn_a_log']) * jax.nn.softplus(dn_a + lw['dn_dt_bias'])
    o_dn, new_state = _gated_delta(dq, dk, dv, beta, g, state0)
    zz = z.reshape(B, T, DN_HEADS, DN_DH)
    o_dn = o_dn * lax.rsqrt(jnp.mean(o_dn * o_dn, -1, keepdims=True) + 1e-6) * lw['dn_norm_w'] * _silu(zz)
    return o_dn.astype(BF16).reshape(B * T, DN_WIDTH), new_state, new_conv


def _cross_attn(q, mem_k, mem_v):
    B, T, _ = q.shape
    q = q.reshape(B, T, MEM_HEADS, MEM_DH)
    s = jnp.einsum('bthd,bmhd->bhtm', q, mem_k, preferred_element_type=F32) * MEM_DH ** -0.5
    p = jax.nn.softmax(s, axis=-1)
    o = jnp.einsum('bhtm,bmhd->bthd', p, mem_v)
    return o.astype(BF16).reshape(B * T, MEM_WIDTH)


def _mixer_and_cross(x, group, lw, lam, lam_init, alpha):
    B, T, D = x.shape
    n = B * T
    tm = _row_tile(T if group['prompt'] else n)
    qb, kf, vf, kb, vb, dn, z, ab = _in_proj(x.reshape(n, D), lw['w_in'], group['rope'],
                                             (T // tm) if group['prompt'] else 1, tm)
    if group['prompt']:
        o_da = _diff_attn_prompt(qb.reshape(B, T, DA_WIDTH), kb.reshape(B, T, DA_WIDTH),
                                 vb.reshape(B, T, DA_WIDTH), lam, lw['da_norm_w'], 1.0 - lam_init)
        o_da = o_da.reshape(n, DA_WIDTH)
    else:
        o = _diff_attn_sample(qb.astype(F32).reshape(B, T, DA_HEADS, 2, DA_DH),
                              kf.reshape(B, T, DA_HEADS, 2, DA_DH), vf.reshape(B, T, DA_HEADS, DA_DV), lam,
                              group['cache_k'], group['cache_v'], group['page_table'])
        o = o * lax.rsqrt(jnp.mean(o * o, -1, keepdims=True) + LN_EPS) * lw['da_norm_w'] * (1.0 - lam_init)
        o_da = o.astype(BF16).reshape(n, DA_WIDTH)
    o_dn, state, conv = _delta_group(dn.reshape(B, T, 3 * DN_WIDTH), ab.reshape(B, T, LANES),
                                     z.reshape(B, T, DN_WIDTH), group['conv0'], group['state0'], lw)
    x1, x1b = _mm_res_ln([o_da, o_dn], [lw['w_out'][:DA_WIDTH], lw['w_out'][DA_WIDTH:]],
                         x.reshape(n, D), lw['ln1_g'], lw['ln1_b'], alpha)
    qc = _matmul(x1b, lw['w_cq'])
    oc = _cross_attn(qc.reshape(B, T, MEM_WIDTH), group['mem_k'], group['mem_v'])
    x2, x2b, logits = _mm_res_ln([oc], [lw['w_co']], x1, lw['ln2_g'], lw['ln2_b'], alpha, lw['w_router'])
    return (x2, x2b, logits, kf.reshape(B, T, DA_HEADS, DA_QK), vf.reshape(B, T, DA_HEADS, DA_DV), state, conv)


def kernel(x_prompt, x_sample, cache_attn_k, cache_attn_v, cache_mem_k, cache_mem_v, state_dn, state_dn_conv,
           page_table, mem_prompt, w_in, da_lambda_q1, da_lambda_k1, da_lambda_q2, da_lambda_k2, da_norm_w,
           dn_conv_w, dn_a_log, dn_dt_bias, dn_norm_w, w_out, ln1_g, ln1_b, w_cq, w_ckv, w_co, ln2_g, ln2_b,
           w_router, router_bias, w_exp_gu, w_exp_down, w_sh_gu, w_sh_down, ln3_g, ln3_b):
    depth = w_in.shape[0]
    alpha = (2 * depth) ** 0.25
    Bp, Sp, D = x_prompt.shape
    Bs, Ts, _ = x_sample.shape
    past = page_table.shape[1] * cache_attn_k.shape[2]
    rope_p = _rope_tables(jnp.arange(Sp, dtype=jnp.int32))
    rope_s = _rope_tables(jnp.tile(past + jnp.arange(Ts, dtype=jnp.int32), Bs))
    yp, ys = x_prompt, x_sample
    outs = [[] for _ in range(10)]
    for l in range(depth):
        wi = w_in[l]
        w_in_perm = jnp.concatenate(
            [wi[:, :IN_MAIN], wi[:, IN_MAIN + IN_AB:], wi[:, IN_MAIN:IN_MAIN + IN_AB],
             jnp.zeros((D, LANES - IN_AB), wi.dtype)], axis=1).astype(BF16)
        lw = {'w_in': w_in_perm, 'da_norm_w': da_norm_w[l], 'dn_conv_w': dn_conv_w[l], 'dn_a_log': dn_a_log[l],
              'dn_dt_bias': dn_dt_bias[l], 'dn_norm_w': dn_norm_w[l], 'w_out': w_out[l].astype(BF16),
              'ln1_g': ln1_g[l], 'ln1_b': ln1_b[l], 'w_cq': w_cq[l].astype(BF16), 'w_co': w_co[l].astype(BF16),
              'ln2_g': ln2_g[l], 'ln2_b': ln2_b[l], 'w_router': w_router[l].astype(BF16)}
        lam_init = 0.8 - 0.6 * math.exp(-0.3 * l)
        lam = (jnp.exp(jnp.sum(da_lambda_q1[l] * da_lambda_k1[l]))
               - jnp.exp(jnp.sum(da_lambda_q2[l] * da_lambda_k2[l])) + lam_init)
        mkv = _matmul(mem_prompt.reshape(-1, D), w_ckv[l].astype(BF16))
        n_mem = mem_prompt.shape[1]
        mk_p = mkv[:, :MEM_WIDTH].reshape(Bp, n_mem, MEM_HEADS, MEM_DH)
        mv_p = mkv[:, MEM_WIDTH:].reshape(Bp, n_mem, MEM_HEADS, MEM_DH)
        grp_p = {'prompt': True, 'rope': rope_p, 'mem_k': mk_p, 'mem_v': mv_p,
                 'conv0': jnp.zeros((Bp, CONV_W - 1, 3 * DN_WIDTH), F32),
                 'state0': jnp.zeros((Bp, DN_HEADS, DN_DH, DN_DH), F32)}
        grp_s = {'prompt': False, 'rope': rope_s, 'mem_k': cache_mem_k[l], 'mem_v': cache_mem_v[l],
                 'conv0': state_dn_conv[l], 'state0': state_dn[l], 'cache_k': cache_attn_k[l],
                 'cache_v': cache_attn_v[l], 'page_table': page_table}
        x2p, x2bp, lgp, kp, vp, sp, cp = _mixer_and_cross(yp, grp_p, lw, lam, lam_init, alpha)
        x2s, x2bs, lgs, ks_, vs_, ss_, cs_ = _mixer_and_cross(ys, grp_s, lw, lam, lam_init, alpha)
        idx, gate = _route(jnp.concatenate([lgp, lgs], 0), router_bias[l])
        routed_p, routed_s = _routed([x2bp, x2bs], idx, gate, w_exp_gu[l].astype(BF16), w_exp_down[l].astype(BF16))
        wsg, wsd = w_sh_gu[l].astype(BF16), w_sh_down[l].astype(BF16)
        yp = _moe_out(x2p, routed_p, wsg, wsd, ln3_g[l], ln3_b[l], alpha).reshape(Bp, Sp, D)
        ys = _moe_out(x2s, routed_s, wsg, wsd, ln3_g[l], ln3_b[l], alpha).reshape(Bs, Ts, D)
        for lst, val in zip(outs, (kp, vp, ks_, vs_, sp, ss_, cp, cs_, mk_p, mv_p)):
            lst.append(val)
    return (yp, ys) + tuple(jnp.stack(o) for o in outs)
```

```python
import functools
import math

import jax
import jax.numpy as jnp
import numpy as np
from jax import lax
from jax.experimental import pallas as pl
from jax.experimental.pallas import tpu as pltpu

D_MODEL = 1024
DA_HEADS = 4
DA_DH = 64
DA_QK = 2 * DA_DH
DA_DV = 2 * DA_DH
DA_WIDTH = DA_HEADS * DA_DV
ROT_DIM = DA_DH // 4
ROPE_THETA = 500000.0
DN_HEADS = 4
DN_DH = 128
DN_WIDTH = DN_HEADS * DN_DH
CONV_W = 4
DN_CHUNK = 64
MEM_HEADS = 4
MEM_DH = 128
MEM_WIDTH = MEM_HEADS * MEM_DH
N_EXPERTS = 256
TOP_K = 8
N_GROUPS = 8
TOPK_GROUPS = 4
D_EXPERT = 256
ROUTED_SCALE = 2.5
LN_EPS = 1e-5
NEG = -1e30

LANES = 128
V7X_VMEM_LIMIT_BYTES = 48 * 1024 * 1024
ROW_TILE = 512
ATTN_TILE = 512
MOE_ROWS = 256
IN_MAIN = 3 * DA_WIDTH + 3 * DN_WIDTH
IN_AB = 2 * DN_HEADS
IN_PERM_COLS = IN_MAIN + DN_WIDTH + LANES

BF16 = jnp.bfloat16
F32 = jnp.float32


def _params(n_axes=1):
    return pltpu.CompilerParams(dimension_semantics=("parallel",) * n_axes,
                                vmem_limit_bytes=V7X_VMEM_LIMIT_BYTES)


def _row_tile(n):
    return ROW_TILE if n % ROW_TILE == 0 else n


def _in_proj_body(x_ref, w_ref, c_ref, sa_ref, sb_ref,
                  q_ref, kf_ref, vf_ref, kb_ref, vb_ref, dn_ref, z_ref, ab_ref):
    h = jnp.dot(x_ref[...].astype(BF16), w_ref[...], preferred_element_type=F32)
    reps = DA_WIDTH // LANES
    c = jnp.concatenate([c_ref[...]] * reps, axis=1)
    sa = jnp.concatenate([sa_ref[...]] * reps, axis=1)
    sb = jnp.concatenate([sb_ref[...]] * reps, axis=1)

    def rope(t):
        return (t * c + pltpu.roll(t, DA_WIDTH - ROT_DIM // 2, 1) * sa
                + pltpu.roll(t, ROT_DIM // 2, 1) * sb)

    q = rope(h[:, 0:DA_WIDTH])
    k = rope(h[:, DA_WIDTH:2 * DA_WIDTH])
    v = h[:, 2 * DA_WIDTH:3 * DA_WIDTH]
    q_ref[...] = (q * DA_DH ** -0.5).astype(BF16)
    kf_ref[...] = k
    vf_ref[...] = v
    kb_ref[...] = k.astype(BF16)
    vb_ref[...] = v.astype(BF16)
    dn_ref[...] = h[:, 3 * DA_WIDTH:IN_MAIN]
    z_ref[...] = h[:, IN_MAIN:IN_MAIN + DN_WIDTH]
    ab_ref[...] = h[:, IN_MAIN + DN_WIDTH:]


def _in_proj(x, w_perm, tables, n_pos_blocks, tm):
    n = x.shape[0]
    row = lambda w: pl.BlockSpec((tm, w), lambda i: (i, 0))
    tab = pl.BlockSpec((tm, LANES), lambda i: (i % n_pos_blocks, 0))
    widths = (DA_WIDTH, DA_WIDTH, DA_WIDTH, DA_WIDTH, DA_WIDTH, 3 * DN_WIDTH, DN_WIDTH, LANES)
    dtypes = (BF16, F32, F32, BF16, BF16, F32, F32, F32)
    return pl.pallas_call(
        _in_proj_body,
        grid=(n // tm,),
        in_specs=[row(D_MODEL), pl.BlockSpec((D_MODEL, IN_PERM_COLS), lambda i: (0, 0)), tab, tab, tab],
        out_specs=[row(w) for w in widths],
        out_shape=[jax.ShapeDtypeStruct((n, w), d) for w, d in zip(widths, dtypes)],
        compiler_params=_params(),
        name="in_proj_rope",
    )(x, w_perm, *tables)


def _rope_tables(pos):
    half = ROT_DIM // 2
    inv = ROPE_THETA ** (-jnp.arange(0, ROT_DIM, 2, dtype=F32) / ROT_DIM)
    ang = pos.astype(F32)[:, None] * inv
    cos, sin = jnp.cos(ang), jnp.sin(ang)
    n = pos.shape[0]
    ones = jnp.ones((n, DA_DH - ROT_DIM), F32)
    zeros = jnp.zeros((n, DA_DH - ROT_DIM), F32)
    zh = jnp.zeros((n, half), F32)
    c = jnp.concatenate([cos, cos, ones], 1)
    sa = jnp.concatenate([-sin, zh, zeros], 1)
    sb = jnp.concatenate([zh, sin, zeros], 1)
    return tuple(jnp.concatenate([t, t], 1) for t in (c, sa, sb))


def _flash_body(lam_ref, q_ref, k_ref, v_ref, nw_ref, o_ref, *, t, out_scale):
    qi = pl.program_id(2)
    q = q_ref[0]
    lane = lax.broadcasted_iota(jnp.int32, q.shape, 1)
    zero = jnp.zeros_like(q)
    qq = jnp.concatenate([jnp.where(lane < DA_DH, q, zero), jnp.where(lane >= DA_DH, q, zero)], axis=0)

    def update(j, carry, masked):
        m, l, acc = carry
        start = pl.multiple_of(j * t, t)
        kb = k_ref[0, pl.ds(start, t), :]
        vb = v_ref[0, pl.ds(start, t), :]
        s = lax.dot_general(qq, kb, (((1,), (1,)), ((), ())), preferred_element_type=F32)
        if masked:
            r = lax.broadcasted_iota(jnp.int32, s.shape, 0)
            c = lax.broadcasted_iota(jnp.int32, s.shape, 1)
            r = jnp.where(r >= t, r - t, r)
            s = jnp.where(c <= r, s, NEG)
        m_new = jnp.maximum(m, jnp.max(s, axis=-1, keepdims=True))
        corr = jnp.exp(m - m_new)
        p = jnp.exp(s - m_new)
        l = l * corr + jnp.sum(p, axis=-1, keepdims=True)
        acc = acc * corr + jnp.dot(p.astype(BF16), vb, preferred_element_type=F32)
        return m_new, l, acc

    init = (jnp.full((2 * t, 1), NEG, F32), jnp.zeros((2 * t, 1), F32), jnp.zeros((2 * t, DA_DV), F32))
    carry = lax.fori_loop(0, qi, lambda j, c: update(j, c, False), init)
    m, l, acc = update(qi, carry, True)
    o = acc / l
    o = o[:t] - lam_ref[0] * o[t:]
    o = o * lax.rsqrt(jnp.mean(o * o, axis=-1, keepdims=True) + LN_EPS) * nw_ref[...] * out_scale
    o_ref[0] = o.astype(o_ref.dtype)


def _diff_attn_prompt(qb, kb, vb, lam, norm_w, out_scale):
    b, s, _ = qb.shape
    t = min(ATTN_TILE, s)
    return pl.pallas_call(
        functools.partial(_flash_body, t=t, out_scale=out_scale),
        grid=(b, DA_HEADS, s // t),
        in_specs=[pl.BlockSpec(memory_space=pltpu.SMEM),
                  pl.BlockSpec((1, t, DA_QK), lambda bi, h, i: (bi, i, h)),
                  pl.BlockSpec((1, s, DA_QK), lambda bi, h, i: (bi, 0, h)),
                  pl.BlockSpec((1, s, DA_DV), lambda bi, h, i: (bi, 0, h)),
                  pl.BlockSpec((1, DA_DV), lambda bi, h, i: (0, 0))],
        out_specs=pl.BlockSpec((1, t, DA_DV), lambda bi, h, i: (bi, i, h)),
        out_shape=jax.ShapeDtypeStruct((b, s, DA_WIDTH), BF16),
        compiler_params=_params(3),
        name="diff_attn_prompt",
    )(lam.reshape(1), qb, kb, vb, norm_w.reshape(1, DA_DV))


def _mm_body(x_ref, w_ref, o_ref):
    o_ref[...] = jnp.dot(x_ref[...].astype(BF16), w_ref[...], preferred_element_type=F32).astype(o_ref.dtype)


def _matmul(x, w, out_dtype=F32):
    m, k = x.shape
    n = w.shape[1]
    tm = _row_tile(m)
    return pl.pallas_call(
        _mm_body,
        grid=(m // tm,),
        in_specs=[pl.BlockSpec((tm, k), lambda i: (i, 0)), pl.BlockSpec((k, n), lambda i: (0, 0))],
        out_specs=pl.BlockSpec((tm, n), lambda i: (i, 0)),
        out_shape=jax.ShapeDtypeStruct((m, n), out_dtype),
        compiler_params=_params(),
        name="dense_matmul",
    )(x, w)


def _ln(x, g, b):
    mu = jnp.mean(x, axis=-1, keepdims=True)
    d = x - mu
    var = jnp.mean(d * d, axis=-1, keepdims=True)
    return d * lax.rsqrt(var + LN_EPS) * g + b


def _mm_res_ln_body(*refs, n_x, alpha, with_logits):
    xs = refs[:n_x]
    ws = refs[n_x:2 * n_x]
    res_ref, g_ref, b_ref = refs[2 * n_x:2 * n_x + 3]
    rest = refs[2 * n_x + 3:]
    h = alpha * res_ref[...]
    for x_ref, w_ref in zip(xs, ws):
        h = h + jnp.dot(x_ref[...].astype(BF16), w_ref[...], preferred_element_type=F32)
    y = _ln(h, g_ref[...], b_ref[...])
    if with_logits:
        wr_ref, y_ref, yb_ref, lg_ref = rest
    else:
        y_ref, yb_ref = rest
    yb = y.astype(BF16)
    y_ref[...] = y
    yb_ref[...] = yb
    if with_logits:
        lg_ref[...] = jnp.dot(yb, wr_ref[...], preferred_element_type=F32)


def _mm_res_ln(xs, ws, res, g, b, alpha, w_router=None):
    m, d = res.shape
    tm = _row_tile(m)
    row = lambda w: pl.BlockSpec((tm, w), lambda i: (i, 0))
    full = lambda a: pl.BlockSpec(a.shape, lambda i: (0, 0))
    g2, b2 = g.reshape(1, d), b.reshape(1, d)
    in_specs = [row(x.shape[1]) for x in xs] + [full(w) for w in ws] + [row(d), full(g2), full(b2)]
    args = list(xs) + list(ws) + [res, g2, b2]
    out_specs = [row(d), row(d)]
    out_shape = [jax.ShapeDtypeStruct((m, d), F32), jax.ShapeDtypeStruct((m, d), BF16)]
    if w_router is not None:
        in_specs.append(full(w_router))
        args.append(w_router)
        out_specs.append(row(w_router.shape[1]))
        out_shape.append(jax.ShapeDtypeStruct((m, w_router.shape[1]), F32))
    return pl.pallas_call(
        functools.partial(_mm_res_ln_body, n_x=len(xs), alpha=alpha, with_logits=w_router is not None),
        grid=(m // tm,),
        in_specs=in_specs, out_specs=out_specs, out_shape=out_shape,
        compiler_params=_params(),
        name="matmul_residual_layernorm",
    )(*args)


def _silu(x):
    return x * jax.nn.sigmoid(x)


def _moe_out_body(x_ref, r_ref, wgu_ref, wd_ref, g_ref, b_ref, y_ref, *, alpha):
    x = x_ref[...]
    h = jnp.dot(x.astype(BF16), wgu_ref[...], preferred_element_type=F32)
    d_sh = h.shape[1] // 2
    a = (_silu(h[:, :d_sh]) * h[:, d_sh:]).astype(BF16)
    shared = jnp.dot(a, wd_ref[...], preferred_element_type=F32)
    y_ref[...] = _ln(alpha * x + (r_ref[...] + shared), g_ref[...], b_ref[...])


def _moe_out(x, routed, w_sh_gu, w_sh_down, g, b, alpha):
    m, d = x.shape
    tm = _row_tile(m)
    row = pl.BlockSpec((tm, d), lambda i: (i, 0))
    full = lambda a: pl.BlockSpec(a.shape, lambda i: (0, 0))
    g2, b2 = g.reshape(1, d), b.reshape(1, d)
    return pl.pallas_call(
        functools.partial(_moe_out_body, alpha=alpha),
        grid=(m // tm,),
        in_specs=[row, row, full(w_sh_gu), full(w_sh_down), full(g2), full(b2)],
        out_specs=row,
        out_shape=jax.ShapeDtypeStruct((m, d), F32),
        compiler_params=_params(),
        name="shared_expert_layernorm",
    )(x, routed, w_sh_gu, w_sh_down, g2, b2)


def _route_body(lg_ref, bias_ref, idx_ref, gate_ref):
    scores = jax.nn.sigmoid(lg_ref[...])
    biased = scores + bias_ref[...]
    lane = lax.broadcasted_iota(jnp.int32, scores.shape, 1)
    lane_f = lane.astype(F32)
    grp = lane // (N_EXPERTS // N_GROUPS)
    neg = -jnp.inf
    big = float(N_EXPERTS)

    def first_argmax(x):
        m = jnp.max(x, axis=-1, keepdims=True)
        return m, jnp.min(jnp.where(x == m, lane_f, big), axis=-1, keepdims=True)

    group_score = []
    for g in range(N_GROUPS):
        xm = jnp.where(grp == g, biased, neg)
        m1, i1 = first_argmax(xm)
        m2 = jnp.max(jnp.where(lane_f == i1, neg, xm), axis=-1, keepdims=True)
        group_score.append(m1 + m2)
    masked = jnp.full(scores.shape, neg, F32)
    for g in range(N_GROUPS):
        rank = jnp.zeros_like(group_score[g])
        for h in range(N_GROUPS):
            if h != g:
                beats = (group_score[h] >= group_score[g]) if h < g else (group_score[h] > group_score[g])
                rank = rank + jnp.where(beats, 1.0, 0.0)
        masked = jnp.where(grp == g, jnp.where(rank < TOPK_GROUPS, biased, neg), masked)
    out_lane = lax.broadcasted_iota(jnp.int32, idx_ref.shape, 1)
    idx_out = jnp.zeros(idx_ref.shape, F32)
    val_out = jnp.zeros(idx_ref.shape, F32)
    for k in range(TOP_K):
        _, i = first_argmax(masked)
        hit = lane_f == i
        v = jnp.sum(jnp.where(hit, scores, 0.0), axis=-1, keepdims=True)
        masked = jnp.where(hit, neg, masked)
        idx_out = jnp.where(out_lane == k, i, idx_out)
        val_out = jnp.where(out_lane == k, v, val_out)
    total = jnp.sum(val_out, axis=-1, keepdims=True)
    idx_ref[...] = idx_out.astype(jnp.int32)
    gate_ref[...] = val_out / total * ROUTED_SCALE


def _route(logits, router_bias):
    n = logits.shape[0]
    tm = _row_tile(n)
    row = lambda w: pl.BlockSpec((tm, w), lambda i: (i, 0))
    idx, gate = pl.pallas_call(
        _route_body,
        grid=(n // tm,),
        in_specs=[row(N_EXPERTS), pl.BlockSpec((1, N_EXPERTS), lambda i: (0, 0))],
        out_specs=[row(LANES), row(LANES)],
        out_shape=[jax.ShapeDtypeStruct((n, LANES), jnp.int32), jax.ShapeDtypeStruct((n, LANES), F32)],
        compiler_params=_params(),
        name="router_topk",
    )(logits, router_bias.reshape(1, N_EXPERTS))
    return idx[:, :TOP_K], gate[:, :TOP_K]


def _experts_body(e_ref, blk_ref, lo_ref, hi_ref, first_ref, x_ref, wgu_ref, wd_ref, o_ref):
    del e_ref, blk_ref
    p = pl.program_id(0)
    lo, hi = lo_ref[p], hi_ref[p]

    def expert_rows():
        h = jnp.dot(x_ref[...], wgu_ref[0], preferred_element_type=F32)
        a = (_silu(h[:, :D_EXPERT]) * h[:, D_EXPERT:]).astype(BF16)
        y = jnp.dot(a, wd_ref[0], preferred_element_type=F32).astype(o_ref.dtype)
        row = lax.broadcasted_iota(jnp.int32, y.shape, 0)
        return y, row

    @pl.when((hi > lo) & (first_ref[p] == 1))
    def _():
        y, row = expert_rows()
        zero = jnp.zeros_like(y)
        o_ref[...] = jnp.where(row >= lo, jnp.where(row < hi, y, zero), zero)

    @pl.when((hi > lo) & (first_ref[p] == 0))
    def _():
        y, row = expert_rows()
        prev = o_ref[...]
        o_ref[...] = jnp.where(row >= lo, jnp.where(row < hi, y, prev), prev)


def _experts(x_sorted, meta, w_gu, w_down):
    n_rows, d = x_sorted.shape
    n_pairs = meta[0].shape[0]
    grid_spec = pltpu.PrefetchScalarGridSpec(
        num_scalar_prefetch=5,
        grid=(n_pairs,),
        in_specs=[pl.BlockSpec((MOE_ROWS, d), lambda p, e, blk, lo, hi, first: (blk[p], 0)),
                  pl.BlockSpec((1, d, 2 * D_EXPERT), lambda p, e, blk, lo, hi, first: (e[p], 0, 0)),
                  pl.BlockSpec((1, D_EXPERT, d), lambda p, e, blk, lo, hi, first: (e[p], 0, 0))],
        out_specs=pl.BlockSpec((MOE_ROWS, d), lambda p, e, blk, lo, hi, first: (blk[p], 0)),
    )
    return pl.pallas_call(
        _experts_body,
        grid_spec=grid_spec,
        out_shape=jax.ShapeDtypeStruct((n_rows, d), BF16),
        compiler_params=pltpu.CompilerParams(dimension_semantics=("arbitrary",),
                                             vmem_limit_bytes=V7X_VMEM_LIMIT_BYTES),
        name="routed_experts",
    )(*meta, x_sorted, w_gu, w_down)


def _routed(xb_parts, idx, gate, w_gu, w_down):
    n_tok = idx.shape[0]
    n_assign = n_tok * TOP_K
    i32 = jnp.int32
    sorted_e, order = lax.sort_key_val(idx.reshape(-1), jnp.arange(n_assign, dtype=i32))
    _, slot_row = lax.sort_key_val(order, jnp.arange(n_assign, dtype=i32))
    bounds = jnp.searchsorted(sorted_e, jnp.arange(N_EXPERTS + 1, dtype=i32), side='left').astype(i32)
    start, count = bounds[:-1], bounds[1:] - bounds[:-1]
    n_blocks = -(-n_assign // MOE_ROWS)
    n_rows = n_blocks * MOE_ROWS
    first_blk = start // MOE_ROWS
    n_blk = jnp.where(count > 0, (start + count - 1) // MOE_ROWS - first_blk + 1, 0)
    pair_end = jnp.cumsum(n_blk)
    pair_start = pair_end - n_blk
    p = jnp.arange(n_blocks + N_EXPERTS - 1, dtype=i32)
    e_p = jnp.minimum(jnp.searchsorted(pair_end, p, side='right'), N_EXPERTS - 1).astype(i32)
    valid = p < pair_end[-1]
    blk_p = jnp.where(valid, first_blk[e_p] + p - pair_start[e_p], n_blocks - 1).astype(i32)
    base = blk_p * MOE_ROWS
    lo_p = jnp.where(valid, jnp.maximum(start[e_p], base) - base, 0).astype(i32)
    hi_p = jnp.where(valid, jnp.minimum(start[e_p] + count[e_p], base + MOE_ROWS) - base, 0).astype(i32)
    first_p = jnp.concatenate([jnp.ones((1,), i32), (blk_p[1:] != blk_p[:-1]).astype(i32)])
    d = xb_parts[0].shape[1]
    x_pad = jnp.concatenate(list(xb_parts) + [jnp.zeros((1, d), BF16)], 0)
    row_tok = jnp.concatenate([order // TOP_K, jnp.full((n_rows - n_assign,), n_tok, i32)])
    y_sorted = _experts(x_pad[row_tok], (e_p, blk_p, lo_p, hi_p, first_p), w_gu, w_down)
    outs, t0 = [], 0
    for part in xb_parts:
        t1 = t0 + part.shape[0]
        rows = slot_row[t0 * TOP_K:t1 * TOP_K]
        y = y_sorted[rows].astype(F32).reshape(t1 - t0, TOP_K, d)
        outs.append(jnp.sum(y * gate[t0:t1, :, None], axis=1))
        t0 = t1
    return outs


def _l2norm(x):
    return x * lax.rsqrt(jnp.sum(x * x, -1, keepdims=True) + 1e-6)


def _softmax_update(carry, s, vals):
    m, l, acc = carry
    m_new = jnp.maximum(m, jnp.max(s, -1))
    corr = jnp.exp(m - m_new)
    p = jnp.exp(s - m_new[..., None])
    l = l * corr + jnp.sum(p, -1)
    acc = acc * corr[..., None] + jnp.einsum('bhctk,bkhd->bhctd', p, vals)
    return (m_new, l, acc)


def _diff_attn_sample(qf, k, v, lam, cache_k, cache_v, page_table):
    B, T = qf.shape[:2]
    page = cache_k.shape[1]

    def page_step(carry, ids):
        kp = cache_k[ids].reshape(B, page, DA_HEADS, 2, DA_DH)
        vp = cache_v[ids]
        s = jnp.einsum('bthcd,bkhcd->bhctk', qf, kp)
        return _softmax_update(carry, s, vp), None

    init = (jnp.full((B, DA_HEADS, 2, T), NEG, F32),
            jnp.zeros((B, DA_HEADS, 2, T), F32),
            jnp.zeros((B, DA_HEADS, 2, T, DA_DV), F32))
    carry, _ = lax.scan(page_step, init, page_table.T)
    s = jnp.einsum('bthcd,bkhcd->bhctk', qf, k)
    mask = jnp.arange(T)[None, :] <= jnp.arange(T)[:, None]
    m, l, acc = _softmax_update(carry, jnp.where(mask, s, NEG), v)
    o = acc / l[..., None]
    o = o[:, :, 0] - lam * o[:, :, 1]
    return jnp.transpose(o, (0, 2, 1, 3))


def _gated_delta(q, k, v, beta, g, state0):
    B, T, H, DK = q.shape
    C = DN_CHUNK if T % DN_CHUNK == 0 else T
    NC = T // C

    def chunks(a):
        a = a.reshape((B, NC, C, H) + a.shape[3:])
        return jnp.moveaxis(a, 3, 2)

    q, k, v, beta, g = chunks(q), chunks(k), chunks(v), chunks(beta), chunks(g)
    G = jnp.cumsum(g, -1)
    incl = jnp.tril(jnp.ones((C, C), bool))
    strict = jnp.tril(jnp.ones((C, C), bool), -1)
    D = jnp.exp(jnp.where(incl, G[..., :, None] - G[..., None, :], -jnp.inf))
    kk = jnp.einsum('...id,...jd->...ij', k, k)
    L = jnp.where(strict, beta[..., :, None] * kk * D, 0.0) + jnp.eye(C, dtype=F32)
    W = lax.linalg.triangular_solve(L, beta[..., None] * v, left_side=True, lower=True, unit_diagonal=True)
    Y = lax.linalg.triangular_solve(L, (beta * jnp.exp(G))[..., None] * k, left_side=True, lower=True,
                                    unit_diagonal=True)
    P = jnp.einsum('...id,...jd->...ij', q, k) * D
    qg = q * jnp.exp(G)[..., None]
    kg = k * jnp.exp(G[..., -1:] - G)[..., None]
    glast = jnp.exp(G[..., -1])

    def step(S, xs):
        Wc, Yc, Pc, qgc, kgc, glc = xs
        U = Wc - Yc @ S
        O = qgc @ S + Pc @ U
        S = glc[..., None, None] * S + jnp.swapaxes(kgc, -1, -2) @ U
        return S, O

    xs = tuple(jnp.moveaxis(a, 1, 0) for a in (W, Y, P, qg, kg, glast))
    S, O = lax.scan(step, state0, xs)
    O = jnp.transpose(O, (1, 0, 3, 2, 4)).reshape(B, T, H, DK)
    return O, S


def _delta_group(dn_qkv, ab, z, conv0, state0, lw):
    B, T, _ = dn_qkv.shape
    xp = jnp.concatenate([conv0, dn_qkv], axis=1)
    cw = lw['dn_conv_w']
    conv = xp[:, 0:T] * cw[0]
    for i in range(1, CONV_W):
        conv = conv + xp[:, i:i + T] * cw[i]
    conv = _silu(conv)
    new_conv = xp[:, T:]
    dq, dk, dv = jnp.split(conv, 3, axis=-1)
    dq = _l2norm(dq.reshape(B, T, DN_HEADS, DN_DH)) * DN_DH ** -0.5
    dk = _l2norm(dk.reshape(B, T, DN_HEADS, DN_DH))
    dv = dv.reshape(B, T, DN_HEADS, DN_DH)
    dn_a, dn_b = ab[..., :DN_HEADS], ab[..., DN_HEADS:IN_AB]
    beta = jax.nn.sigmoid(dn_b)
    g = -jnp.exp(lw['dn_a_log']) * jax.nn.softplus(dn_a + lw['dn_dt_bias'])
    o_dn, new_state = _gated_delta(dq, dk, dv, beta, g, state0)
    zz = z.reshape(B, T, DN_HEADS, DN_DH)
    o_dn = o_dn * lax.rsqrt(jnp.mean(o_dn * o_dn, -1, keepdims=True) + 1e-6) * lw['dn_norm_w'] * _silu(zz)
    return o_dn.astype(BF16).reshape(B * T, DN_WIDTH), new_state, new_conv


def _cross_attn(q, mem_k, mem_v):
    B, T, _ = q.shape
    q = q.reshape(B, T, MEM_HEADS, MEM_DH)
    s = jnp.einsum('bthd,bmhd->bhtm', q, mem_k, preferred_element_type=F32) * MEM_DH ** -0.5
    p = jax.nn.softmax(s, axis=-1)
    o = jnp.einsum('bhtm,bmhd->bthd', p, mem_v)
    return o.astype(BF16).reshape(B * T, MEM_WIDTH)


def _mixer_and_cross(x, group, lw, lam, lam_init, alpha):
    B, T, D = x.shape
    n = B * T
    tm = _row_tile(T if group['prompt'] else n)
    qb, kf, vf, kb, vb, dn, z, ab = _in_proj(x.reshape(n, D), lw['w_in'], group['rope'],
                                             (T // tm) if group['prompt'] else 1, tm)
    if group['prompt']:
        o_da = _diff_attn_prompt(qb.reshape(B, T, DA_WIDTH), kb.reshape(B, T, DA_WIDTH),
                                 vb.reshape(B, T, DA_WIDTH), lam, lw['da_norm_w'], 1.0 - lam_init)
        o_da = o_da.reshape(n, DA_WIDTH)
    else:
        o = _diff_attn_sample(qb.astype(F32).reshape(B, T, DA_HEADS, 2, DA_DH),
                              kf.reshape(B, T, DA_HEADS, 2, DA_DH), vf.reshape(B, T, DA_HEADS, DA_DV), lam,
                              group['cache_k'], group['cache_v'], group['page_table'])
        o = o * lax.rsqrt(jnp.mean(o * o, -1, keepdims=True) + LN_EPS) * lw['da_norm_w'] * (1.0 - lam_init)
        o_da = o.astype(BF16).reshape(n, DA_WIDTH)
    o_dn, state, conv = _delta_group(dn.reshape(B, T, 3 * DN_WIDTH), ab.reshape(B, T, LANES),
                                     z.reshape(B, T, DN_WIDTH), group['conv0'], group['state0'], lw)
    x1, x1b = _mm_res_ln([o_da, o_dn], [lw['w_out'][:DA_WIDTH], lw['w_out'][DA_WIDTH:]],
                         x.reshape(n, D), lw['ln1_g'], lw['ln1_b'], alpha)
    qc = _matmul(x1b, lw['w_cq'])
    oc = _cross_attn(qc.reshape(B, T, MEM_WIDTH), group['mem_k'], group['mem_v'])
    x2, x2b, logits = _mm_res_ln([oc], [lw['w_co']], x1, lw['ln2_g'], lw['ln2_b'], alpha, lw['w_router'])
    return (x2, x2b, logits, kf.reshape(B, T, DA_HEADS, DA_QK), vf.reshape(B, T, DA_HEADS, DA_DV), state, conv)


def kernel(x_prompt, x_sample, cache_attn_k, cache_attn_v, cache_mem_k, cache_mem_v, state_dn, state_dn_conv,
           page_table, mem_prompt, w_in, da_lambda_q1, da_lambda_k1, da_lambda_q2, da_lambda_k2, da_norm_w,
           dn_conv_w, dn_a_log, dn_dt_bias, dn_norm_w, w_out, ln1_g, ln1_b, w_cq, w_ckv, w_co, ln2_g, ln2_b,
           w_router, router_bias, w_exp_gu, w_exp_down, w_sh_gu, w_sh_down, ln3_g, ln3_b):
    depth = w_in.shape[0]
    alpha = (2 * depth) ** 0.25
    Bp, Sp, D = x_prompt.shape
    Bs, Ts, _ = x_sample.shape
    past = page_table.shape[1] * cache_attn_k.shape[2]
    rope_p = _rope_tables(jnp.arange(Sp, dtype=jnp.int32))
    rope_s = _rope_tables(jnp.tile(past + jnp.arange(Ts, dtype=jnp.int32), Bs))
    yp, ys = x_prompt, x_sample
    outs = [[] for _ in range(10)]
    for l in range(depth):
        wi = w_in[l]
        w_in_perm = jnp.concatenate(
            [wi[:, :IN_MAIN], wi[:, IN_MAIN + IN_AB:], wi[:, IN_MAIN:IN_MAIN + IN_AB],
             jnp.zeros((D, LANES - IN_AB), wi.dtype)], axis=1).astype(BF16)
        lw = {'w_in': w_in_perm, 'da_norm_w': da_norm_w[l], 'dn_conv_w': dn_conv_w[l], 'dn_a_log': dn_a_log[l],
              'dn_dt_bias': dn_dt_bias[l], 'dn_norm_w': dn_norm_w[l], 'w_out': w_out[l].astype(BF16),
              'ln1_g': ln1_g[l], 'ln1_b': ln1_b[l], 'w_cq': w_cq[l].astype(BF16), 'w_co': w_co[l].astype(BF16),
              'ln2_g': ln2_g[l], 'ln2_b': ln2_b[l], 'w_router': w_router[l].astype(BF16)}
        lam_init = 0.8 - 0.6 * math.exp(-0.3 * l)
        lam = (jnp.exp(jnp.sum(da_lambda_q1[l] * da_lambda_k1[l]))
               - jnp.exp(jnp.sum(da_lambda_q2[l] * da_lambda_k2[l])) + lam_init)
        mkv = _matmul(mem_prompt.reshape(-1, D), w_ckv[l].astype(BF16))
        n_mem = mem_prompt.shape[1]
        mk_p = mkv[:, :MEM_WIDTH].reshape(Bp, n_mem, MEM_HEADS, MEM_DH)
        mv_p = mkv[:, MEM_WIDTH:].reshape(Bp, n_mem, MEM_HEADS, MEM_DH)
        grp_p = {'prompt': True, 'rope': rope_p, 'mem_k': mk_p, 'mem_v': mv_p,
                 'conv0': jnp.zeros((Bp, CONV_W - 1, 3 * DN_WIDTH), F32),
                 'state0': jnp.zeros((Bp, DN_HEADS, DN_DH, DN_DH), F32)}
        grp_s = {'prompt': False, 'rope': rope_s, 'mem_k': cache_mem_k[l], 'mem_v': cache_mem_v[l],
                 'conv0': state_dn_conv[l], 'state0': state_dn[l], 'cache_k': cache_attn_k[l],
                 'cache_v': cache_attn_v[l], 'page_table': page_table}
        x2p, x2bp, lgp, kp, vp, sp, cp = _mixer_and_cross(yp, grp_p, lw, lam, lam_init, alpha)
        x2s, x2bs, lgs, ks_, vs_, ss_, cs_ = _mixer_and_cross(ys, grp_s, lw, lam, lam_init, alpha)
        idx, gate = _route(jnp.concatenate([lgp, lgs], 0), router_bias[l])
        routed_p, routed_s = _routed([x2bp, x2bs], idx, gate, w_exp_gu[l].astype(BF16), w_exp_down[l].astype(BF16))
        wsg, wsd = w_sh_gu[l].astype(BF16), w_sh_down[l].astype(BF16)
        yp = _moe_out(x2p, routed_p, wsg, wsd, ln3_g[l], ln3_b[l], alpha).reshape(Bp, Sp, D)
        ys = _moe_out(x2s, routed_s, wsg, wsd, ln3_g[l], ln3_b[l], alpha).reshape(Bs, Ts, D)
        for lst, val in zip(outs, (kp, vp, ks_, vs_, sp, ss_, cp, cs_, mk_p, mv_p)):
            lst.append(val)
    return (yp, ys) + tuple(jnp.stack(o) for o in outs)
```

```python
import functools
import math

import jax
import jax.numpy as jnp
import numpy as np
from jax import lax
from jax.experimental import pallas as pl
from jax.experimental.pallas import tpu as pltpu

D_MODEL = 1024
DA_HEADS = 4
DA_DH = 64
DA_QK = 2 * DA_DH
DA_DV = 2 * DA_DH
DA_WIDTH = DA_HEADS * DA_DV
ROT_DIM = DA_DH // 4
ROPE_THETA = 500000.0
DN_HEADS = 4
DN_DH = 128
DN_WIDTH = DN_HEADS * DN_DH
CONV_W = 4
DN_CHUNK = 64
MEM_HEADS = 4
MEM_DH = 128
MEM_WIDTH = MEM_HEADS * MEM_DH
N_EXPERTS = 256
TOP_K = 8
N_GROUPS = 8
TOPK_GROUPS = 4
D_EXPERT = 256
ROUTED_SCALE = 2.5
LN_EPS = 1e-5
NEG = -1e30

LANES = 128
V7X_VMEM_LIMIT_BYTES = 48 * 1024 * 1024
ROW_TILE = 512
ATTN_TILE = 512
MOE_ROWS = 256
IN_MAIN = 3 * DA_WIDTH + 3 * DN_WIDTH
IN_AB = 2 * DN_HEADS
IN_PERM_COLS = IN_MAIN + DN_WIDTH + LANES

BF16 = jnp.bfloat16
F32 = jnp.float32


def _params(n_axes=1):
    return pltpu.CompilerParams(dimension_semantics=("parallel",) * n_axes,
                                vmem_limit_bytes=V7X_VMEM_LIMIT_BYTES)


def _row_tile(n):
    return ROW_TILE if n % ROW_TILE == 0 else n


def _in_proj_body(x_ref, w_ref, c_ref, sa_ref, sb_ref,
                  q_ref, kf_ref, vf_ref, kb_ref, vb_ref, dn_ref, z_ref, ab_ref):
    h = jnp.dot(x_ref[...].astype(BF16), w_ref[...], preferred_element_type=F32)
    reps = DA_WIDTH // LANES
    c = jnp.concatenate([c_ref[...]] * reps, axis=1)
    sa = jnp.concatenate([sa_ref[...]] * reps, axis=1)
    sb = jnp.concatenate([sb_ref[...]] * reps, axis=1)

    def rope(t):
        return (t * c + pltpu.roll(t, DA_WIDTH - ROT_DIM // 2, 1) * sa
                + pltpu.roll(t, ROT_DIM // 2, 1) * sb)

    q = rope(h[:, 0:DA_WIDTH])
    k = rope(h[:, DA_WIDTH:2 * DA_WIDTH])
    v = h[:, 2 * DA_WIDTH:3 * DA_WIDTH]
    q_ref[...] = (q * DA_DH ** -0.5).astype(BF16)
    kf_ref[...] = k
    vf_ref[...] = v
    kb_ref[...] = k.astype(BF16)
    vb_ref[...] = v.astype(BF16)
    dn_ref[...] = h[:, 3 * DA_WIDTH:IN_MAIN]
    z_ref[...] = h[:, IN_MAIN:IN_MAIN + DN_WIDTH]
    ab_ref[...] = h[:, IN_MAIN + DN_WIDTH:]


def _in_proj(x, w_perm, tables, n_pos_blocks, tm):
    n = x.shape[0]
    row = lambda w: pl.BlockSpec((tm, w), lambda i: (i, 0))
    tab = pl.BlockSpec((tm, LANES), lambda i: (i % n_pos_blocks, 0))
    widths = (DA_WIDTH, DA_WIDTH, DA_WIDTH, DA_WIDTH, DA_WIDTH, 3 * DN_WIDTH, DN_WIDTH, LANES)
    dtypes = (BF16, F32, F32, BF16, BF16, F32, F32, F32)
    return pl.pallas_call(
        _in_proj_body,
        grid=(n // tm,),
        in_specs=[row(D_MODEL), pl.BlockSpec((D_MODEL, IN_PERM_COLS), lambda i: (0, 0)), tab, tab, tab],
        out_specs=[row(w) for w in widths],
        out_shape=[jax.ShapeDtypeStruct((n, w), d) for w, d in zip(widths, dtypes)],
        compiler_params=_params(),
        name="in_proj_rope",
    )(x, w_perm, *tables)


def _rope_tables(pos):
    half = ROT_DIM // 2
    inv = ROPE_THETA ** (-jnp.arange(0, ROT_DIM, 2, dtype=F32) / ROT_DIM)
    ang = pos.astype(F32)[:, None] * inv
    cos, sin = jnp.cos(ang), jnp.sin(ang)
    n = pos.shape[0]
    ones = jnp.ones((n, DA_DH - ROT_DIM), F32)
    zeros = jnp.zeros((n, DA_DH - ROT_DIM), F32)
    zh = jnp.zeros((n, half), F32)
    c = jnp.concatenate([cos, cos, ones], 1)
    sa = jnp.concatenate([-sin, zh, zeros], 1)
    sb = jnp.concatenate([zh, sin, zeros], 1)
    return tuple(jnp.concatenate([t, t], 1) for t in (c, sa, sb))


def _flash_body(lam_ref, q_ref, k_ref, v_ref, nw_ref, o_ref, *, t, out_scale):
    qi = pl.program_id(2)
    q = q_ref[0]
    lane = lax.broadcasted_iota(jnp.int32, q.shape, 1)
    zero = jnp.zeros_like(q)
    qq = jnp.concatenate([jnp.where(lane < DA_DH, q, zero), jnp.where(lane >= DA_DH, q, zero)], axis=0)

    def update(j, carry, masked):
        m, l, acc = carry
        start = pl.multiple_of(j * t, t)
        kb = k_ref[0, pl.ds(start, t), :]
        vb = v_ref[0, pl.ds(start, t), :]
        s = lax.dot_general(qq, kb, (((1,), (1,)), ((), ())), preferred_element_type=F32)
        if masked:
            r = lax.broadcasted_iota(jnp.int32, s.shape, 0)
            c = lax.broadcasted_iota(jnp.int32, s.shape, 1)
            r = jnp.where(r >= t, r - t, r)
            s = jnp.where(c <= r, s, NEG)
        m_new = jnp.maximum(m, jnp.max(s, axis=-1, keepdims=True))
        corr = jnp.exp(m - m_new)
        p = jnp.exp(s - m_new)
        l = l * corr + jnp.sum(p, axis=-1, keepdims=True)
        acc = acc * corr + jnp.dot(p.astype(BF16), vb, preferred_element_type=F32)
        return m_new, l, acc

    init = (jnp.full((2 * t, 1), NEG, F32), jnp.zeros((2 * t, 1), F32), jnp.zeros((2 * t, DA_DV), F32))
    carry = lax.fori_loop(0, qi, lambda j, c: update(j, c, False), init)
    m, l, acc = update(qi, carry, True)
    o = acc / l
    o = o[:t] - lam_ref[0] * o[t:]
    o = o * lax.rsqrt(jnp.mean(o * o, axis=-1, keepdims=True) + LN_EPS) * nw_ref[...] * out_scale
    o_ref[0] = o.astype(o_ref.dtype)


def _diff_attn_prompt(qb, kb, vb, lam, norm_w, out_scale):
    b, s, _ = qb.shape
    t = min(ATTN_TILE, s)
    return pl.pallas_call(
        functools.partial(_flash_body, t=t, out_scale=out_scale),
        grid=(b, DA_HEADS, s // t),
        in_specs=[pl.BlockSpec(memory_space=pltpu.SMEM),
                  pl.BlockSpec((1, t, DA_QK), lambda bi, h, i: (bi, i, h)),
                  pl.BlockSpec((1, s, DA_QK), lambda bi, h, i: (bi, 0, h)),
                  pl.BlockSpec((1, s, DA_DV), lambda bi, h, i: (bi, 0, h)),
                  pl.BlockSpec((1, DA_DV), lambda bi, h, i: (0, 0))],
        out_specs=pl.BlockSpec((1, t, DA_DV), lambda bi, h, i: (bi, i, h)),
        out_shape=jax.ShapeDtypeStruct((b, s, DA_WIDTH), BF16),
        compiler_params=_params(3),
        name="diff_attn_prompt",
    )(lam.reshape(1), qb, kb, vb, norm_w.reshape(1, DA_DV))


def _mm_body(x_ref, w_ref, o_ref):
    o_ref[...] = jnp.dot(x_ref[...].astype(BF16), w_ref[...], preferred_element_type=F32).astype(o_ref.dtype)


def _matmul(x, w, out_dtype=F32):
    m, k = x.shape
    n = w.shape[1]
    tm = _row_tile(m)
    return pl.pallas_call(
        _mm_body,
        grid=(m // tm,),
        in_specs=[pl.BlockSpec((tm, k), lambda i: (i, 0)), pl.BlockSpec((k, n), lambda i: (0, 0))],
        out_specs=pl.BlockSpec((tm, n), lambda i: (i, 0)),
        out_shape=jax.ShapeDtypeStruct((m, n), out_dtype),
        compiler_params=_params(),
        name="dense_matmul",
    )(x, w)


def _ln(x, g, b):
    mu = jnp.mean(x, axis=-1, keepdims=True)
    d = x - mu
    var = jnp.mean(d * d, axis=-1, keepdims=True)
    return d * lax.rsqrt(var + LN_EPS) * g + b


def _mm_res_ln_body(*refs, n_x, alpha, with_logits):
    xs = refs[:n_x]
    ws = refs[n_x:2 * n_x]
    res_ref, g_ref, b_ref = refs[2 * n_x:2 * n_x + 3]
    rest = refs[2 * n_x + 3:]
    h = alpha * res_ref[...]
    for x_ref, w_ref in zip(xs, ws):
        h = h + jnp.dot(x_ref[...].astype(BF16), w_ref[...], preferred_element_type=F32)
    y = _ln(h, g_ref[...], b_ref[...])
    if with_logits:
        wr_ref, y_ref, yb_ref, lg_ref = rest
    else:
        y_ref, yb_ref = rest
    yb = y.astype(BF16)
    y_ref[...] = y
    yb_ref[...] = yb
    if with_logits:
        lg_ref[...] = jnp.dot(yb, wr_ref[...], preferred_element_type=F32)


def _mm_res_ln(xs, ws, res, g, b, alpha, w_router=None):
    m, d = res.shape
    tm = _row_tile(m)
    row = lambda w: pl.BlockSpec((tm, w), lambda i: (i, 0))
    full = lambda a: pl.BlockSpec(a.shape, lambda i: (0, 0))
    g2, b2 = g.reshape(1, d), b.reshape(1, d)
    in_specs = [row(x.shape[1]) for x in xs] + [full(w) for w in ws] + [row(d), full(g2), full(b2)]
    args = list(xs) + list(ws) + [res, g2, b2]
    out_specs = [row(d), row(d)]
    out_shape = [jax.ShapeDtypeStruct((m, d), F32), jax.ShapeDtypeStruct((m, d), BF16)]
    if w_router is not None:
        in_specs.append(full(w_router))
        args.append(w_router)
        out_specs.append(row(w_router.shape[1]))
        out_shape.append(jax.ShapeDtypeStruct((m, w_router.shape[1]), F32))
    return pl.pallas_call(
        functools.partial(_mm_res_ln_body, n_x=len(xs), alpha=alpha, with_logits=w_router is not None),
        grid=(m // tm,),
        in_specs=in_specs, out_specs=out_specs, out_shape=out_shape,
        compiler_params=_params(),
        name="matmul_residual_layernorm",
    )(*args)


def _silu(x):
    return x * jax.nn.sigmoid(x)


def _moe_out_body(x_ref, r_ref, wgu_ref, wd_ref, g_ref, b_ref, y_ref, *, alpha):
    x = x_ref[...]
    h = jnp.dot(x.astype(BF16), wgu_ref[...], preferred_element_type=F32)
    d_sh = h.shape[1] // 2
    a = (_silu(h[:, :d_sh]) * h[:, d_sh:]).astype(BF16)
    shared = jnp.dot(a, wd_ref[...], preferred_element_type=F32)
    y_ref[...] = _ln(alpha * x + (r_ref[...] + shared), g_ref[...], b_ref[...])


def _moe_out(x, routed, w_sh_gu, w_sh_down, g, b, alpha):
    m, d = x.shape
    tm = _row_tile(m)
    row = pl.BlockSpec((tm, d), lambda i: (i, 0))
    full = lambda a: pl.BlockSpec(a.shape, lambda i: (0, 0))
    g2, b2 = g.reshape(1, d), b.reshape(1, d)
    return pl.pallas_call(
        functools.partial(_moe_out_body, alpha=alpha),
        grid=(m // tm,),
        in_specs=[row, row, full(w_sh_gu), full(w_sh_down), full(g2), full(b2)],
        out_specs=row,
        out_shape=jax.ShapeDtypeStruct((m, d), F32),
        compiler_params=_params(),
        name="shared_expert_layernorm",
    )(x, routed, w_sh_gu, w_sh_down, g2, b2)


def _route_body(lg_ref, bias_ref, idx_ref, gate_ref):
    scores = jax.nn.sigmoid(lg_ref[...])
    biased = scores + bias_ref[...]
    lane = lax.broadcasted_iota(jnp.int32, scores.shape, 1)
    lane_f = lane.astype(F32)
    grp = lane // (N_EXPERTS // N_GROUPS)
    neg = -jnp.inf
    big = float(N_EXPERTS)

    def first_argmax(x):
        m = jnp.max(x, axis=-1, keepdims=True)
        return m, jnp.min(jnp.where(x == m, lane_f, big), axis=-1, keepdims=True)

    group_score = []
    for g in range(N_GROUPS):
        xm = jnp.where(grp == g, biased, neg)
        m1, i1 = first_argmax(xm)
        m2 = jnp.max(jnp.where(lane_f == i1, neg, xm), axis=-1, keepdims=True)
        group_score.append(m1 + m2)
    masked = jnp.full(scores.shape, neg, F32)
    for g in range(N_GROUPS):
        rank = jnp.zeros_like(group_score[g])
        for h in range(N_GROUPS):
            if h != g:
                beats = (group_score[h] >= group_score[g]) if h < g else (group_score[h] > group_score[g])
                rank = rank + jnp.where(beats, 1.0, 0.0)
        masked = jnp.where(grp == g, jnp.where(rank < TOPK_GROUPS, biased, neg), masked)
    out_lane = lax.broadcasted_iota(jnp.int32, idx_ref.shape, 1)
    idx_out = jnp.zeros(idx_ref.shape, F32)
    val_out = jnp.zeros(idx_ref.shape, F32)
    for k in range(TOP_K):
        _, i = first_argmax(masked)
        hit = lane_f == i
        v = jnp.sum(jnp.where(hit, scores, 0.0), axis=-1, keepdims=True)
        masked = jnp.where(hit, neg, masked)
        idx_out = jnp.where(out_lane == k, i, idx_out)
        val_out = jnp.where(out_lane == k, v, val_out)
    total = jnp.sum(val_out, axis=-1, keepdims=True)
    idx_ref[...] = idx_out.astype(jnp.int32)
    gate_ref[...] = val_out / total * ROUTED_SCALE


def _route(logits, router_bias):
    n = logits.shape[0]
    tm = _row_tile(n)
    row = lambda w: pl.BlockSpec((tm, w), lambda i: (i, 0))
    idx, gate = pl.pallas_call(
        _route_body,
        grid=(n // tm,),
        in_specs=[row(N_EXPERTS), pl.BlockSpec((1, N_EXPERTS), lambda i: (0, 0))],
        out_specs=[row(LANES), row(LANES)],
        out_shape=[jax.ShapeDtypeStruct((n, LANES), jnp.int32), jax.ShapeDtypeStruct((n, LANES), F32)],
        compiler_params=_params(),
        name="router_topk",
    )(logits, router_bias.reshape(1, N_EXPERTS))
    return idx[:, :TOP_K], gate[:, :TOP_K]


def _experts_body(e_ref, blk_ref, lo_ref, hi_ref, first_ref, x_ref, wgu_ref, wd_ref, o_ref):
    del e_ref, blk_ref
    p = pl.program_id(0)
    lo, hi = lo_ref[p], hi_ref[p]

    def expert_rows():
        h = jnp.dot(x_ref[...], wgu_ref[0], preferred_element_type=F32)
        a = (_silu(h[:, :D_EXPERT]) * h[:, D_EXPERT:]).astype(BF16)
        y = jnp.dot(a, wd_ref[0], preferred_element_type=F32).astype(o_ref.dtype)
        row = lax.broadcasted_iota(jnp.int32, y.shape, 0)
        return y, row

    @pl.when((hi > lo) & (first_ref[p] == 1))
    def _():
        y, row = expert_rows()
        zero = jnp.zeros_like(y)
        o_ref[...] = jnp.where(row >= lo, jnp.where(row < hi, y, zero), zero)

    @pl.when((hi > lo) & (first_ref[p] == 0))
    def _():
        y, row = expert_rows()
        prev = o_ref[...]
        o_ref[...] = jnp.where(row >= lo, jnp.where(row < hi, y, prev), prev)


def _experts(x_sorted, meta, w_gu, w_down):
    n_rows, d = x_sorted.shape
    n_pairs = meta[0].shape[0]
    grid_spec = pltpu.PrefetchScalarGridSpec(
        num_scalar_prefetch=5,
        grid=(n_pairs,),
        in_specs=[pl.BlockSpec((MOE_ROWS, d), lambda p, e, blk, lo, hi, first: (blk[p], 0)),
                  pl.BlockSpec((1, d, 2 * D_EXPERT), lambda p, e, blk, lo, hi, first: (e[p], 0, 0)),
                  pl.BlockSpec((1, D_EXPERT, d), lambda p, e, blk, lo, hi, first: (e[p], 0, 0))],
        out_specs=pl.BlockSpec((MOE_ROWS, d), lambda p, e, blk, lo, hi, first: (blk[p], 0)),
    )
    return pl.pallas_call(
        _experts_body,
        grid_spec=grid_spec,
        out_shape=jax.ShapeDtypeStruct((n_rows, d), BF16),
        compiler_params=pltpu.CompilerParams(dimension_semantics=("arbitrary",),
                                             vmem_limit_bytes=V7X_VMEM_LIMIT_BYTES),
        name="routed_experts",
    )(*meta, x_sorted, w_gu, w_down)


def _routed(xb_parts, idx, gate, w_gu, w_down):
    n_tok = idx.shape[0]
    n_assign = n_tok * TOP_K
    i32 = jnp.int32
    sorted_e, order = lax.sort_key_val(idx.reshape(-1), jnp.arange(n_assign, dtype=i32))
    _, slot_row = lax.sort_key_val(order, jnp.arange(n_assign, dtype=i32))
    bounds = jnp.searchsorted(sorted_e, jnp.arange(N_EXPERTS + 1, dtype=i32), side='left').astype(i32)
    start, count = bounds[:-1], bounds[1:] - bounds[:-1]
    n_blocks = -(-n_assign // MOE_ROWS)
    n_rows = n_blocks * MOE_ROWS
    first_blk = start // MOE_ROWS
    n_blk = jnp.where(count > 0, (start + count - 1) // MOE_ROWS - first_blk + 1, 0)
    pair_end = jnp.cumsum(n_blk)
    pair_start = pair_end - n_blk
    p = jnp.arange(n_blocks + N_EXPERTS - 1, dtype=i32)
    e_p = jnp.minimum(jnp.searchsorted(pair_end, p, side='right'), N_EXPERTS - 1).astype(i32)
    valid = p < pair_end[-1]
    blk_p = jnp.where(valid, first_blk[e_p] + p - pair_start[e_p], n_blocks - 1).astype(i32)
    base = blk_p * MOE_ROWS
    lo_p = jnp.where(valid, jnp.maximum(start[e_p], base) - base, 0).astype(i32)
    hi_p = jnp.where(valid, jnp.minimum(start[e_p] + count[e_p], base + MOE_ROWS) - base, 0).astype(i32)
    first_p = jnp.concatenate([jnp.ones((1,), i32), (blk_p[1:] != blk_p[:-1]).astype(i32)])
    d = xb_parts[0].shape[1]
    x_pad = jnp.concatenate(list(xb_parts) + [jnp.zeros((1, d), BF16)], 0)
    row_tok = jnp.concatenate([order // TOP_K, jnp.full((n_rows - n_assign,), n_tok, i32)])
    y_sorted = _experts(x_pad[row_tok], (e_p, blk_p, lo_p, hi_p, first_p), w_gu, w_down)
    outs, t0 = [], 0
    for part in xb_parts:
        t1 = t0 + part.shape[0]
        rows = slot_row[t0 * TOP_K:t1 * TOP_K]
        y = y_sorted[rows].astype(F32).reshape(t1 - t0, TOP_K, d)
        outs.append(jnp.sum(y * gate[t0:t1, :, None], axis=1))
        t0 = t1
    return outs


def _l2norm(x):
    return x * lax.rsqrt(jnp.sum(x * x, -1, keepdims=True) + 1e-6)


def _paged_body(pt_ref, q_ref, kn_ref, vn_ref, mask_ref, *refs, n_pg):
    del pt_ref
    k_refs, v_refs = refs[:n_pg], refs[n_pg:2 * n_pg]
    o_ref, m_ref, l_ref, acc_ref = refs[2 * n_pg:]
    j = pl.program_id(1)
    q = q_ref[0]
    nt = (((1,), (1,)), ((), ()))

    @pl.when(j == 0)
    def _():
        m_ref[...] = jnp.full(m_ref.shape, NEG, F32)
        l_ref[...] = jnp.zeros(l_ref.shape, F32)
        acc_ref[...] = jnp.zeros(acc_ref.shape, F32)

    def update(scores, values):
        s = scores[0] if len(scores) == 1 else jnp.concatenate(scores, axis=1)
        m_prev = m_ref[...]
        m_new = jnp.maximum(m_prev, jnp.max(s, axis=-1, keepdims=True))
        corr = jnp.exp(m_prev - m_new)
        p = jnp.exp(s - m_new)
        l_ref[...] = l_ref[...] * corr + jnp.sum(p, axis=-1, keepdims=True)
        acc = acc_ref[...] * corr
        w = values[0].shape[0]
        for i, v in enumerate(values):
            acc = acc + jnp.dot(p[:, i * w:(i + 1) * w].astype(BF16), v, preferred_element_type=F32)
        acc_ref[...] = acc
        m_ref[...] = m_new

    ks = [r[0].astype(BF16) for r in k_refs]
    vs = [r[0].astype(BF16) for r in v_refs]
    update([lax.dot_general(q, k, nt, preferred_element_type=F32) for k in ks], vs)

    @pl.when(j == pl.num_programs(1) - 1)
    def _():
        s = lax.dot_general(q, kn_ref[0], nt, preferred_element_type=F32)
        update([jnp.where(mask_ref[...] > 0.5, s, NEG)], [vn_ref[0]])
        o_ref[0] = acc_ref[...] / l_ref[...]


def _diff_attn_sample(qb, kb, vb, lam, cache_k, cache_v, page_table):
    B, T, W = qb.shape
    n_pool, page = cache_k.shape[:2]
    n_pages = page_table.shape[1]
    n_pg = max(c for c in (4, 2, 1) if n_pages % c == 0)
    rows = DA_HEADS * 2 * T
    r = jnp.arange(rows)
    lane_grp = jnp.arange(W) // DA_DH
    q_rows = jnp.where((lane_grp[None, :] == (r // T)[:, None])[None], jnp.tile(qb, (1, rows // T, 1)), 0)
    pad = lambda a: jnp.pad(a, ((0, 0), (0, page - T), (0, 0)))
    mask = (jnp.arange(page)[None, :] <= (r % T)[:, None]).astype(F32)
    ck = cache_k.reshape(n_pool, page, W)
    cv = cache_v.reshape(n_pool, page, W)
    per_seq = lambda shape: pl.BlockSpec(shape, lambda b, j, pt: (b, 0, 0))
    page_spec = lambda i: pl.BlockSpec((1, page, W), lambda b, j, pt: (pt[b * n_pages + j * n_pg + i], 0, 0))
    grid_spec = pltpu.PrefetchScalarGridSpec(
        num_scalar_prefetch=1,
        grid=(B, n_pages // n_pg),
        in_specs=[per_seq((1, rows, W)), per_seq((1, page, W)), per_seq((1, page, W)),
                  pl.BlockSpec((rows, page), lambda b, j, pt: (0, 0))]
                 + [page_spec(i) for i in range(n_pg)] * 2,
        out_specs=per_seq((1, rows, W)),
        scratch_shapes=[pltpu.VMEM((rows, 1), F32), pltpu.VMEM((rows, 1), F32), pltpu.VMEM((rows, W), F32)],
    )
    o = pl.pallas_call(
        functools.partial(_paged_body, n_pg=n_pg),
        grid_spec=grid_spec,
        out_shape=jax.ShapeDtypeStruct((B, rows, W), F32),
        compiler_params=pltpu.CompilerParams(dimension_semantics=("parallel", "arbitrary"),
                                             vmem_limit_bytes=V7X_VMEM_LIMIT_BYTES),
        name="diff_attn_paged",
    )(page_table.reshape(-1), q_rows, pad(kb), pad(vb), mask, *([ck] * n_pg), *([cv] * n_pg))
    o = o.reshape(B, DA_HEADS, 2, T, DA_HEADS, DA_DV)
    o = jnp.stack([o[:, h, :, :, h] for h in range(DA_HEADS)], axis=1)
    o = o[:, :, 0] - lam * o[:, :, 1]
    return jnp.transpose(o, (0, 2, 1, 3))


def _gated_delta(q, k, v, beta, g, state0):
    B, T, H, DK = q.shape
    C = DN_CHUNK if T % DN_CHUNK == 0 else T
    NC = T // C

    def chunks(a):
        a = a.reshape((B, NC, C, H) + a.shape[3:])
        return jnp.moveaxis(a, 3, 2)

    q, k, v, beta, g = chunks(q), chunks(k), chunks(v), chunks(beta), chunks(g)
    G = jnp.cumsum(g, -1)
    incl = jnp.tril(jnp.ones((C, C), bool))
    strict = jnp.tril(jnp.ones((C, C), bool), -1)
    D = jnp.exp(jnp.where(incl, G[..., :, None] - G[..., None, :], -jnp.inf))
    kk = jnp.einsum('...id,...jd->...ij', k, k)
    L = jnp.where(strict, beta[..., :, None] * kk * D, 0.0) + jnp.eye(C, dtype=F32)
    W = lax.linalg.triangular_solve(L, beta[..., None] * v, left_side=True, lower=True, unit_diagonal=True)
    Y = lax.linalg.triangular_solve(L, (beta * jnp.exp(G))[..., None] * k, left_side=True, lower=True,
                                    unit_diagonal=True)
    P = jnp.einsum('...id,...jd->...ij', q, k) * D
    qg = q * jnp.exp(G)[..., None]
    kg = k * jnp.exp(G[..., -1:] - G)[..., None]
    glast = jnp.exp(G[..., -1])

    def step(S, xs):
        Wc, Yc, Pc, qgc, kgc, glc = xs
        U = Wc - Yc @ S
        O = qgc @ S + Pc @ U
        S = glc[..., None, None] * S + jnp.swapaxes(kgc, -1, -2) @ U
        return S, O

    xs = tuple(jnp.moveaxis(a, 1, 0) for a in (W, Y, P, qg, kg, glast))
    S, O = lax.scan(step, state0, xs)
    O = jnp.transpose(O, (1, 0, 3, 2, 4)).reshape(B, T, H, DK)
    return O, S


def _gdn_body(q_ref, k_ref, v_ref, col_ref, row_ref, s0_ref, o_ref, sf_ref, s_scr, *, n_chunks):
    j = pl.program_id(1)
    c_len = DN_CHUNK
    nn, nt, tn = ((2,), (1,)), ((2,), (2,)), ((1,), (1,))

    @pl.when(j == 0)
    def _():
        s_scr[...] = s0_ref[0]

    ii = lax.broadcasted_iota(jnp.int32, (1, c_len, c_len), 1)
    jj = lax.broadcasted_iota(jnp.int32, (1, c_len, c_len), 2)
    incl, strict = ii >= jj, ii > jj
    eye = jnp.where(ii == jj, 1.0, 0.0)
    bmm = lambda a, b, dims: lax.dot_general(a.astype(BF16), b.astype(BF16), (dims, ((0,), (0,))),
                                             preferred_element_type=F32)
    items = [(c, h) for c in range(n_chunks) for h in range(DN_HEADS)]
    rows = lambda c: slice(c * c_len, (c + 1) * c_len)
    lanes = lambda h: slice(h * DN_DH, (h + 1) * DN_DH)
    q = jnp.stack([q_ref[0, rows(c), lanes(h)] for c, h in items])
    k = jnp.stack([k_ref[0, rows(c), lanes(h)] for c, h in items])
    v = jnp.stack([v_ref[0, rows(c), lanes(h)] for c, h in items])
    beta = jnp.stack([col_ref[0, rows(c), h:h + 1] for c, h in items])
    g_col = jnp.stack([col_ref[0, rows(c), DN_HEADS + h:DN_HEADS + h + 1] for c, h in items])
    g_row = jnp.stack([row_ref[0, c, h:h + 1, :] for c, h in items])
    g_last = g_row[:, :, c_len - 1:c_len]
    decay = jnp.where(incl, jnp.exp(jnp.where(incl, g_col - g_row, 0.0)), 0.0)
    a = jnp.where(strict, beta * bmm(k, k, nt) * decay, 0.0)
    inv = eye - a
    power = a
    for _ in range(5):
        power = bmm(power, power, nn)
        inv = inv + bmm(inv, power, nn)
    e_g = jnp.exp(g_col)
    wy = bmm(inv, jnp.concatenate([beta * v, (beta * e_g) * k], axis=2), nn)
    w, y = wy[:, :, :DN_DH], wy[:, :, DN_DH:]
    p = bmm(q, k, nt) * decay
    qg = q * e_g
    kg = k * jnp.exp(g_last - g_col)
    g_tile = jnp.exp(g_last)
    state = s_scr[...]
    for c in range(n_chunks):
        sl = slice(c * DN_HEADS, (c + 1) * DN_HEADS)
        u = w[sl] - bmm(y[sl], state, nn)
        o = bmm(qg[sl], state, nn) + bmm(p[sl], u, nn)
        state = g_tile[sl] * state + bmm(kg[sl], u, tn)
        for h in range(DN_HEADS):
            o_ref[0, rows(c), lanes(h)] = o[h]
    s_scr[...] = state

    @pl.when(j == pl.num_programs(1) - 1)
    def _():
        sf_ref[0] = s_scr[...]


def _gated_delta_chunked(q, k, v, beta, g, state0):
    B, T, W = q.shape
    n_chunks = max(c for c in (4, 2, 1) if (T // DN_CHUNK) % c == 0)
    tile = n_chunks * DN_CHUNK
    G = jnp.cumsum(g.reshape(B, T // DN_CHUNK, DN_CHUNK, DN_HEADS), axis=2)
    col = jnp.concatenate([beta, G.reshape(B, T, DN_HEADS), jnp.zeros((B, T, LANES - 2 * DN_HEADS), F32)], -1)
    row = jnp.pad(jnp.swapaxes(G, 2, 3), ((0, 0), (0, 0), (0, 8 - DN_HEADS), (0, 0)))
    seq = lambda w: pl.BlockSpec((1, tile, w), lambda b, j: (b, j, 0))
    st = pl.BlockSpec((1, DN_HEADS, DN_DH, DN_DH), lambda b, j: (b, 0, 0, 0))
    return pl.pallas_call(
        functools.partial(_gdn_body, n_chunks=n_chunks),
        grid=(B, T // tile),
        in_specs=[seq(W), seq(W), seq(W), seq(LANES),
                  pl.BlockSpec((1, n_chunks, 8, DN_CHUNK), lambda b, j: (b, j, 0, 0)), st],
        out_specs=[seq(W), st],
        out_shape=[jax.ShapeDtypeStruct((B, T, W), F32),
                   jax.ShapeDtypeStruct((B, DN_HEADS, DN_DH, DN_DH), F32)],
        scratch_shapes=[pltpu.VMEM((DN_HEADS, DN_DH, DN_DH), F32)],
        compiler_params=pltpu.CompilerParams(dimension_semantics=("parallel", "arbitrary"),
                                             vmem_limit_bytes=V7X_VMEM_LIMIT_BYTES),
        name="gated_delta_chunked",
    )(q, k, v, col, row, state0)


def _delta_group(dn_qkv, ab, z, conv0, state0, lw):
    B, T, _ = dn_qkv.shape
    xp = jnp.concatenate([conv0, dn_qkv], axis=1)
    cw = lw['dn_conv_w']
    conv = xp[:, 0:T] * cw[0]
    for i in range(1, CONV_W):
        conv = conv + xp[:, i:i + T] * cw[i]
    conv = _silu(conv)
    new_conv = xp[:, T:]
    dq, dk, dv = jnp.split(conv, 3, axis=-1)
    dq = _l2norm(dq.reshape(B, T, DN_HEADS, DN_DH)) * DN_DH ** -0.5
    dk = _l2norm(dk.reshape(B, T, DN_HEADS, DN_DH))
    dv = dv.reshape(B, T, DN_HEADS, DN_DH)
    dn_a, dn_b = ab[..., :DN_HEADS], ab[..., DN_HEADS:IN_AB]
    beta = jax.nn.sigmoid(dn_b)
    g = -jnp.exp(lw['dn_a_log']) * jax.nn.softplus(dn_a + lw['dn_dt_bias'])
    if T % DN_CHUNK == 0:
        o_dn, new_state = _gated_delta_chunked(dq.reshape(B, T, DN_WIDTH), dk.reshape(B, T, DN_WIDTH),
                                               dv.reshape(B, T, DN_WIDTH), beta, g, state0)
        o_dn = o_dn.reshape(B, T, DN_HEADS, DN_DH)
    else:
        o_dn, new_state = _gated_delta(dq, dk, dv, beta, g, state0)
    zz = z.reshape(B, T, DN_HEADS, DN_DH)
    o_dn = o_dn * lax.rsqrt(jnp.mean(o_dn * o_dn, -1, keepdims=True) + 1e-6) * lw['dn_norm_w'] * _silu(zz)
    return o_dn.astype(BF16).reshape(B * T, DN_WIDTH), new_state, new_conv


def _cross_attn(q, mem_k, mem_v):
    B, T, _ = q.shape
    q = q.reshape(B, T, MEM_HEADS, MEM_DH)
    s = jnp.einsum('bthd,bmhd->bhtm', q, mem_k, preferred_element_type=F32) * MEM_DH ** -0.5
    p = jax.nn.softmax(s, axis=-1)
    o = jnp.einsum('bhtm,bmhd->bthd', p, mem_v)
    return o.astype(BF16).reshape(B * T, MEM_WIDTH)


def _mixer_and_cross(x, group, lw, lam, lam_init, alpha):
    B, T, D = x.shape
    n = B * T
    tm = _row_tile(T if group['prompt'] else n)
    qb, kf, vf, kb, vb, dn, z, ab = _in_proj(x.reshape(n, D), lw['w_in'], group['rope'],
                                             (T // tm) if group['prompt'] else 1, tm)
    if group['prompt']:
        o_da = _diff_attn_prompt(qb.reshape(B, T, DA_WIDTH), kb.reshape(B, T, DA_WIDTH),
                                 vb.reshape(B, T, DA_WIDTH), lam, lw['da_norm_w'], 1.0 - lam_init)
        o_da = o_da.reshape(n, DA_WIDTH)
    else:
        o = _diff_attn_sample(qb.reshape(B, T, DA_WIDTH), kb.reshape(B, T, DA_WIDTH), vb.reshape(B, T, DA_WIDTH),
                              lam, group['cache_k'], group['cache_v'], group['page_table'])
        o = o * lax.rsqrt(jnp.mean(o * o, -1, keepdims=True) + LN_EPS) * lw['da_norm_w'] * (1.0 - lam_init)
        o_da = o.astype(BF16).reshape(n, DA_WIDTH)
    o_dn, state, conv = _delta_group(dn.reshape(B, T, 3 * DN_WIDTH), ab.reshape(B, T, LANES),
                                     z.reshape(B, T, DN_WIDTH), group['conv0'], group['state0'], lw)
    x1, x1b = _mm_res_ln([o_da, o_dn], [lw['w_out'][:DA_WIDTH], lw['w_out'][DA_WIDTH:]],
                         x.reshape(n, D), lw['ln1_g'], lw['ln1_b'], alpha)
    qc = _matmul(x1b, lw['w_cq'])
    oc = _cross_attn(qc.reshape(B, T, MEM_WIDTH), group['mem_k'], group['mem_v'])
    x2, x2b, logits = _mm_res_ln([oc], [lw['w_co']], x1, lw['ln2_g'], lw['ln2_b'], alpha, lw['w_router'])
    return (x2, x2b, logits, kf.reshape(B, T, DA_HEADS, DA_QK), vf.reshape(B, T, DA_HEADS, DA_DV), state, conv)


def kernel(x_prompt, x_sample, cache_attn_k, cache_attn_v, cache_mem_k, cache_mem_v, state_dn, state_dn_conv,
           page_table, mem_prompt, w_in, da_lambda_q1, da_lambda_k1, da_lambda_q2, da_lambda_k2, da_norm_w,
           dn_conv_w, dn_a_log, dn_dt_bias, dn_norm_w, w_out, ln1_g, ln1_b, w_cq, w_ckv, w_co, ln2_g, ln2_b,
           w_router, router_bias, w_exp_gu, w_exp_down, w_sh_gu, w_sh_down, ln3_g, ln3_b):
    depth = w_in.shape[0]
    alpha = (2 * depth) ** 0.25
    Bp, Sp, D = x_prompt.shape
    Bs, Ts, _ = x_sample.shape
    past = page_table.shape[1] * cache_attn_k.shape[2]
    rope_p = _rope_tables(jnp.arange(Sp, dtype=jnp.int32))
    rope_s = _rope_tables(jnp.tile(past + jnp.arange(Ts, dtype=jnp.int32), Bs))
    yp, ys = x_prompt, x_sample
    outs = [[] for _ in range(10)]
    for l in range(depth):
        wi = w_in[l]
        w_in_perm = jnp.concatenate(
            [wi[:, :IN_MAIN], wi[:, IN_MAIN + IN_AB:], wi[:, IN_MAIN:IN_MAIN + IN_AB],
             jnp.zeros((D, LANES - IN_AB), wi.dtype)], axis=1).astype(BF16)
        lw = {'w_in': w_in_perm, 'da_norm_w': da_norm_w[l], 'dn_conv_w': dn_conv_w[l], 'dn_a_log': dn_a_log[l],
              'dn_dt_bias': dn_dt_bias[l], 'dn_norm_w': dn_norm_w[l], 'w_out': w_out[l].astype(BF16),
              'ln1_g': ln1_g[l], 'ln1_b': ln1_b[l], 'w_cq': w_cq[l].astype(BF16), 'w_co': w_co[l].astype(BF16),
              'ln2_g': ln2_g[l], 'ln2_b': ln2_b[l], 'w_router': w_router[l].astype(BF16)}
        lam_init = 0.8 - 0.6 * math.exp(-0.3 * l)
        lam = (jnp.exp(jnp.sum(da_lambda_q1[l] * da_lambda_k1[l]))
               - jnp.exp(jnp.sum(da_lambda_q2[l] * da_lambda_k2[l])) + lam_init)
        mkv = _matmul(mem_prompt.reshape(-1, D), w_ckv[l].astype(BF16))
        n_mem = mem_prompt.shape[1]
        mk_p = mkv[:, :MEM_WIDTH].reshape(Bp, n_mem, MEM_HEADS, MEM_DH)
        mv_p = mkv[:, MEM_WIDTH:].reshape(Bp, n_mem, MEM_HEADS, MEM_DH)
        grp_p = {'prompt': True, 'rope': rope_p, 'mem_k': mk_p, 'mem_v': mv_p,
                 'conv0': jnp.zeros((Bp, CONV_W - 1, 3 * DN_WIDTH), F32),
                 'state0': jnp.zeros((Bp, DN_HEADS, DN_DH, DN_DH), F32)}
        grp_s = {'prompt': False, 'rope': rope_s, 'mem_k': cache_mem_k[l], 'mem_v': cache_mem_v[l],
                 'conv0': state_dn_conv[l], 'state0': state_dn[l], 'cache_k': cache_attn_k[l],
                 'cache_v': cache_attn_v[l], 'page_table': page_table}
        x2p, x2bp, lgp, kp, vp, sp, cp = _mixer_and_cross(yp, grp_p, lw, lam, lam_init, alpha)
        x2s, x2bs, lgs, ks_, vs_, ss_, cs_ = _mixer_and_cross(ys, grp_s, lw, lam, lam_init, alpha)
        idx, gate = _route(jnp.concatenate([lgp, lgs], 0), router_bias[l])
        routed_p, routed_s = _routed([x2bp, x2bs], idx, gate, w_exp_gu[l].astype(BF16), w_exp_down[l].astype(BF16))
        wsg, wsd = w_sh_gu[l].astype(BF16), w_sh_down[l].astype(BF16)
        yp = _moe_out(x2p, routed_p, wsg, wsd, ln3_g[l], ln3_b[l], alpha).reshape(Bp, Sp, D)
        ys = _moe_out(x2s, routed_s, wsg, wsd, ln3_g[l], ln3_b[l], alpha).reshape(Bs, Ts, D)
        for lst, val in zip(outs, (kp, vp, ks_, vs_, sp, ss_, cp, cs_, mk_p, mv_p)):
            lst.append(val)
    return (yp, ys) + tuple(jnp.stack(o) for o in outs)
```

```python
import functools
import math

import jax
import jax.numpy as jnp
import numpy as np
from jax import lax
from jax.experimental import pallas as pl
from jax.experimental.pallas import tpu as pltpu

D_MODEL = 1024
DA_HEADS = 4
DA_DH = 64
DA_QK = 2 * DA_DH
DA_DV = 2 * DA_DH
DA_WIDTH = DA_HEADS * DA_DV
ROT_DIM = DA_DH // 4
ROPE_THETA = 500000.0
DN_HEADS = 4
DN_DH = 128
DN_WIDTH = DN_HEADS * DN_DH
CONV_W = 4
DN_CHUNK = 64
MEM_HEADS = 4
MEM_DH = 128
MEM_WIDTH = MEM_HEADS * MEM_DH
N_EXPERTS = 256
TOP_K = 8
N_GROUPS = 8
TOPK_GROUPS = 4
D_EXPERT = 256
ROUTED_SCALE = 2.5
LN_EPS = 1e-5
NEG = -1e30

LANES = 128
V7X_VMEM_LIMIT_BYTES = 48 * 1024 * 1024
ROW_TILE = 512
ATTN_TILE = 512
MOE_ROWS = 256
IN_MAIN = 3 * DA_WIDTH + 3 * DN_WIDTH
IN_AB = 2 * DN_HEADS
IN_PERM_COLS = IN_MAIN + DN_WIDTH + LANES

BF16 = jnp.bfloat16
F32 = jnp.float32


def _params(n_axes=1):
    return pltpu.CompilerParams(dimension_semantics=("parallel",) * n_axes,
                                vmem_limit_bytes=V7X_VMEM_LIMIT_BYTES)


def _row_tile(n):
    return ROW_TILE if n % ROW_TILE == 0 else n


def _in_proj_body(x_ref, w_ref, c_ref, sa_ref, sb_ref,
                  q_ref, kf_ref, vf_ref, kb_ref, vb_ref, dn_ref, z_ref, ab_ref):
    h = jnp.dot(x_ref[...].astype(BF16), w_ref[...], preferred_element_type=F32)
    reps = DA_WIDTH // LANES
    c = jnp.concatenate([c_ref[...]] * reps, axis=1)
    sa = jnp.concatenate([sa_ref[...]] * reps, axis=1)
    sb = jnp.concatenate([sb_ref[...]] * reps, axis=1)

    def rope(t):
        return (t * c + pltpu.roll(t, DA_WIDTH - ROT_DIM // 2, 1) * sa
                + pltpu.roll(t, ROT_DIM // 2, 1) * sb)

    q = rope(h[:, 0:DA_WIDTH])
    k = rope(h[:, DA_WIDTH:2 * DA_WIDTH])
    v = h[:, 2 * DA_WIDTH:3 * DA_WIDTH]
    q_ref[...] = (q * DA_DH ** -0.5).astype(BF16)
    kf_ref[...] = k
    vf_ref[...] = v
    kb_ref[...] = k.astype(BF16)
    vb_ref[...] = v.astype(BF16)
    dn_ref[...] = h[:, 3 * DA_WIDTH:IN_MAIN]
    z_ref[...] = h[:, IN_MAIN:IN_MAIN + DN_WIDTH]
    ab_ref[...] = h[:, IN_MAIN + DN_WIDTH:]


def _in_proj(x, w_perm, tables, n_pos_blocks, tm):
    n = x.shape[0]
    row = lambda w: pl.BlockSpec((tm, w), lambda i: (i, 0))
    tab = pl.BlockSpec((tm, LANES), lambda i: (i % n_pos_blocks, 0))
    widths = (DA_WIDTH, DA_WIDTH, DA_WIDTH, DA_WIDTH, DA_WIDTH, 3 * DN_WIDTH, DN_WIDTH, LANES)
    dtypes = (BF16, F32, F32, BF16, BF16, F32, F32, F32)
    return pl.pallas_call(
        _in_proj_body,
        grid=(n // tm,),
        in_specs=[row(D_MODEL), pl.BlockSpec((D_MODEL, IN_PERM_COLS), lambda i: (0, 0)), tab, tab, tab],
        out_specs=[row(w) for w in widths],
        out_shape=[jax.ShapeDtypeStruct((n, w), d) for w, d in zip(widths, dtypes)],
        compiler_params=_params(),
        name="in_proj_rope",
    )(x, w_perm, *tables)


def _rope_tables(pos):
    half = ROT_DIM // 2
    inv = ROPE_THETA ** (-jnp.arange(0, ROT_DIM, 2, dtype=F32) / ROT_DIM)
    ang = pos.astype(F32)[:, None] * inv
    cos, sin = jnp.cos(ang), jnp.sin(ang)
    n = pos.shape[0]
    ones = jnp.ones((n, DA_DH - ROT_DIM), F32)
    zeros = jnp.zeros((n, DA_DH - ROT_DIM), F32)
    zh = jnp.zeros((n, half), F32)
    c = jnp.concatenate([cos, cos, ones], 1)
    sa = jnp.concatenate([-sin, zh, zeros], 1)
    sb = jnp.concatenate([zh, sin, zeros], 1)
    return tuple(jnp.concatenate([t, t], 1) for t in (c, sa, sb))


def _flash_body(lam_ref, q_ref, k_ref, v_ref, nw_ref, o_ref, *, t, out_scale):
    qi = pl.program_id(2)
    q = q_ref[0]
    lane = lax.broadcasted_iota(jnp.int32, q.shape, 1)
    zero = jnp.zeros_like(q)
    qq = jnp.concatenate([jnp.where(lane < DA_DH, q, zero), jnp.where(lane >= DA_DH, q, zero)], axis=0)

    def update(j, carry, masked):
        m, l, acc = carry
        start = pl.multiple_of(j * t, t)
        kb = k_ref[0, pl.ds(start, t), :]
        vb = v_ref[0, pl.ds(start, t), :]
        s = lax.dot_general(qq, kb, (((1,), (1,)), ((), ())), preferred_element_type=F32)
        if masked:
            r = lax.broadcasted_iota(jnp.int32, s.shape, 0)
            c = lax.broadcasted_iota(jnp.int32, s.shape, 1)
            r = jnp.where(r >= t, r - t, r)
            s = jnp.where(c <= r, s, NEG)
        m_new = jnp.maximum(m, jnp.max(s, axis=-1, keepdims=True))
        corr = jnp.exp(m - m_new)
        p = jnp.exp(s - m_new)
        l = l * corr + jnp.sum(p, axis=-1, keepdims=True)
        acc = acc * corr + jnp.dot(p.astype(BF16), vb, preferred_element_type=F32)
        return m_new, l, acc

    init = (jnp.full((2 * t, 1), NEG, F32), jnp.zeros((2 * t, 1), F32), jnp.zeros((2 * t, DA_DV), F32))
    carry = lax.fori_loop(0, qi, lambda j, c: update(j, c, False), init)
    m, l, acc = update(qi, carry, True)
    o = acc / l
    o = o[:t] - lam_ref[0] * o[t:]
    o = o * lax.rsqrt(jnp.mean(o * o, axis=-1, keepdims=True) + LN_EPS) * nw_ref[...] * out_scale
    o_ref[0] = o.astype(o_ref.dtype)


def _diff_attn_prompt(qb, kb, vb, lam, norm_w, out_scale):
    b, s, _ = qb.shape
    t = min(ATTN_TILE, s)
    return pl.pallas_call(
        functools.partial(_flash_body, t=t, out_scale=out_scale),
        grid=(b, DA_HEADS, s // t),
        in_specs=[pl.BlockSpec(memory_space=pltpu.SMEM),
                  pl.BlockSpec((1, t, DA_QK), lambda bi, h, i: (bi, i, h)),
                  pl.BlockSpec((1, s, DA_QK), lambda bi, h, i: (bi, 0, h)),
                  pl.BlockSpec((1, s, DA_DV), lambda bi, h, i: (bi, 0, h)),
                  pl.BlockSpec((1, DA_DV), lambda bi, h, i: (0, 0))],
        out_specs=pl.BlockSpec((1, t, DA_DV), lambda bi, h, i: (bi, i, h)),
        out_shape=jax.ShapeDtypeStruct((b, s, DA_WIDTH), BF16),
        compiler_params=_params(3),
        name="diff_attn_prompt",
    )(lam.reshape(1), qb, kb, vb, norm_w.reshape(1, DA_DV))


def _mm_body(x_ref, w_ref, o_ref):
    o_ref[...] = jnp.dot(x_ref[...].astype(BF16), w_ref[...], preferred_element_type=F32).astype(o_ref.dtype)


def _matmul(x, w, out_dtype=F32):
    m, k = x.shape
    n = w.shape[1]
    tm = _row_tile(m)
    return pl.pallas_call(
        _mm_body,
        grid=(m // tm,),
        in_specs=[pl.BlockSpec((tm, k), lambda i: (i, 0)), pl.BlockSpec((k, n), lambda i: (0, 0))],
        out_specs=pl.BlockSpec((tm, n), lambda i: (i, 0)),
        out_shape=jax.ShapeDtypeStruct((m, n), out_dtype),
        compiler_params=_params(),
        name="dense_matmul",
    )(x, w)


def _ln(x, g, b):
    mu = jnp.mean(x, axis=-1, keepdims=True)
    d = x - mu
    var = jnp.mean(d * d, axis=-1, keepdims=True)
    return d * lax.rsqrt(var + LN_EPS) * g + b


def _mm_res_ln_body(*refs, n_x, alpha, with_logits):
    xs = refs[:n_x]
    ws = refs[n_x:2 * n_x]
    res_ref, g_ref, b_ref = refs[2 * n_x:2 * n_x + 3]
    rest = refs[2 * n_x + 3:]
    h = alpha * res_ref[...]
    for x_ref, w_ref in zip(xs, ws):
        h = h + jnp.dot(x_ref[...].astype(BF16), w_ref[...], preferred_element_type=F32)
    y = _ln(h, g_ref[...], b_ref[...])
    if with_logits:
        wr_ref, y_ref, yb_ref, lg_ref = rest
    else:
        y_ref, yb_ref = rest
    yb = y.astype(BF16)
    y_ref[...] = y
    yb_ref[...] = yb
    if with_logits:
        lg_ref[...] = jnp.dot(yb, wr_ref[...], preferred_element_type=F32)


def _mm_res_ln(xs, ws, res, g, b, alpha, w_router=None):
    m, d = res.shape
    tm = _row_tile(m)
    row = lambda w: pl.BlockSpec((tm, w), lambda i: (i, 0))
    full = lambda a: pl.BlockSpec(a.shape, lambda i: (0, 0))
    g2, b2 = g.reshape(1, d), b.reshape(1, d)
    in_specs = [row(x.shape[1]) for x in xs] + [full(w) for w in ws] + [row(d), full(g2), full(b2)]
    args = list(xs) + list(ws) + [res, g2, b2]
    out_specs = [row(d), row(d)]
    out_shape = [jax.ShapeDtypeStruct((m, d), F32), jax.ShapeDtypeStruct((m, d), BF16)]
    if w_router is not None:
        in_specs.append(full(w_router))
        args.append(w_router)
        out_specs.append(row(w_router.shape[1]))
        out_shape.append(jax.ShapeDtypeStruct((m, w_router.shape[1]), F32))
    return pl.pallas_call(
        functools.partial(_mm_res_ln_body, n_x=len(xs), alpha=alpha, with_logits=w_router is not None),
        grid=(m // tm,),
        in_specs=in_specs, out_specs=out_specs, out_shape=out_shape,
        compiler_params=_params(),
        name="matmul_residual_layernorm",
    )(*args)


def _silu(x):
    return x * jax.nn.sigmoid(x)


def _moe_out_body(x_ref, r_ref, wgu_ref, wd_ref, g_ref, b_ref, y_ref, *, alpha):
    x = x_ref[...]
    h = jnp.dot(x.astype(BF16), wgu_ref[...], preferred_element_type=F32)
    d_sh = h.shape[1] // 2
    a = (_silu(h[:, :d_sh]) * h[:, d_sh:]).astype(BF16)
    shared = jnp.dot(a, wd_ref[...], preferred_element_type=F32)
    y_ref[...] = _ln(alpha * x + (r_ref[...] + shared), g_ref[...], b_ref[...])


def _moe_out(x, routed, w_sh_gu, w_sh_down, g, b, alpha):
    m, d = x.shape
    tm = _row_tile(m)
    row = pl.BlockSpec((tm, d), lambda i: (i, 0))
    full = lambda a: pl.BlockSpec(a.shape, lambda i: (0, 0))
    g2, b2 = g.reshape(1, d), b.reshape(1, d)
    return pl.pallas_call(
        functools.partial(_moe_out_body, alpha=alpha),
        grid=(m // tm,),
        in_specs=[row, row, full(w_sh_gu), full(w_sh_down), full(g2), full(b2)],
        out_specs=row,
        out_shape=jax.ShapeDtypeStruct((m, d), F32),
        compiler_params=_params(),
        name="shared_expert_layernorm",
    )(x, routed, w_sh_gu, w_sh_down, g2, b2)


def _route_body(lg_ref, bias_ref, idx_ref, gate_ref):
    scores = jax.nn.sigmoid(lg_ref[...])
    biased = scores + bias_ref[...]
    lane = lax.broadcasted_iota(jnp.int32, scores.shape, 1)
    lane_f = lane.astype(F32)
    grp = lane // (N_EXPERTS // N_GROUPS)
    neg = -jnp.inf
    big = float(N_EXPERTS)

    def first_argmax(x):
        m = jnp.max(x, axis=-1, keepdims=True)
        return m, jnp.min(jnp.where(x == m, lane_f, big), axis=-1, keepdims=True)

    group_score = []
    for g in range(N_GROUPS):
        xm = jnp.where(grp == g, biased, neg)
        m1, i1 = first_argmax(xm)
        m2 = jnp.max(jnp.where(lane_f == i1, neg, xm), axis=-1, keepdims=True)
        group_score.append(m1 + m2)
    masked = jnp.full(scores.shape, neg, F32)
    for g in range(N_GROUPS):
        rank = jnp.zeros_like(group_score[g])
        for h in range(N_GROUPS):
            if h != g:
                beats = (group_score[h] >= group_score[g]) if h < g else (group_score[h] > group_score[g])
                rank = rank + jnp.where(beats, 1.0, 0.0)
        masked = jnp.where(grp == g, jnp.where(rank < TOPK_GROUPS, biased, neg), masked)
    out_lane = lax.broadcasted_iota(jnp.int32, idx_ref.shape, 1)
    idx_out = jnp.zeros(idx_ref.shape, F32)
    val_out = jnp.zeros(idx_ref.shape, F32)
    for k in range(TOP_K):
        _, i = first_argmax(masked)
        hit = lane_f == i
        v = jnp.sum(jnp.where(hit, scores, 0.0), axis=-1, keepdims=True)
        masked = jnp.where(hit, neg, masked)
        idx_out = jnp.where(out_lane == k, i, idx_out)
        val_out = jnp.where(out_lane == k, v, val_out)
    total = jnp.sum(val_out, axis=-1, keepdims=True)
    idx_ref[...] = idx_out.astype(jnp.int32)
    gate_ref[...] = val_out / total * ROUTED_SCALE


def _route(logits, router_bias):
    n = logits.shape[0]
    tm = _row_tile(n)
    row = lambda w: pl.BlockSpec((tm, w), lambda i: (i, 0))
    idx, gate = pl.pallas_call(
        _route_body,
        grid=(n // tm,),
        in_specs=[row(N_EXPERTS), pl.BlockSpec((1, N_EXPERTS), lambda i: (0, 0))],
        out_specs=[row(LANES), row(LANES)],
        out_shape=[jax.ShapeDtypeStruct((n, LANES), jnp.int32), jax.ShapeDtypeStruct((n, LANES), F32)],
        compiler_params=_params(),
        name="router_topk",
    )(logits, router_bias.reshape(1, N_EXPERTS))
    return idx[:, :TOP_K], gate[:, :TOP_K]


def _experts_body(e_ref, blk_ref, lo_ref, hi_ref, first_ref, x_ref, wgu_ref, wd_ref, o_ref):
    del e_ref, blk_ref
    p = pl.program_id(0)
    lo, hi = lo_ref[p], hi_ref[p]

    def expert_rows():
        h = jnp.dot(x_ref[...], wgu_ref[0], preferred_element_type=F32)
        a = (_silu(h[:, :D_EXPERT]) * h[:, D_EXPERT:]).astype(BF16)
        y = jnp.dot(a, wd_ref[0], preferred_element_type=F32).astype(o_ref.dtype)
        row = lax.broadcasted_iota(jnp.int32, y.shape, 0)
        return y, row

    @pl.when((hi > lo) & (first_ref[p] == 1))
    def _():
        y, row = expert_rows()
        zero = jnp.zeros_like(y)
        o_ref[...] = jnp.where(row >= lo, jnp.where(row < hi, y, zero), zero)

    @pl.when((hi > lo) & (first_ref[p] == 0))
    def _():
        y, row = expert_rows()
        prev = o_ref[...]
        o_ref[...] = jnp.where(row >= lo, jnp.where(row < hi, y, prev), prev)


def _experts(x_sorted, meta, w_gu, w_down):
    n_rows, d = x_sorted.shape
    n_pairs = meta[0].shape[0]
    grid_spec = pltpu.PrefetchScalarGridSpec(
        num_scalar_prefetch=5,
        grid=(n_pairs,),
        in_specs=[pl.BlockSpec((MOE_ROWS, d), lambda p, e, blk, lo, hi, first: (blk[p], 0)),
                  pl.BlockSpec((1, d, 2 * D_EXPERT), lambda p, e, blk, lo, hi, first: (e[p], 0, 0)),
                  pl.BlockSpec((1, D_EXPERT, d), lambda p, e, blk, lo, hi, first: (e[p], 0, 0))],
        out_specs=pl.BlockSpec((MOE_ROWS, d), lambda p, e, blk, lo, hi, first: (blk[p], 0)),
    )
    return pl.pallas_call(
        _experts_body,
        grid_spec=grid_spec,
        out_shape=jax.ShapeDtypeStruct((n_rows, d), BF16),
        compiler_params=pltpu.CompilerParams(dimension_semantics=("arbitrary",),
                                             vmem_limit_bytes=V7X_VMEM_LIMIT_BYTES),
        name="routed_experts",
    )(*meta, x_sorted, w_gu, w_down)


def _routed(xb_parts, idx, gate, w_gu, w_down):
    n_tok = idx.shape[0]
    n_assign = n_tok * TOP_K
    i32 = jnp.int32
    sorted_e, order = lax.sort_key_val(idx.reshape(-1), jnp.arange(n_assign, dtype=i32))
    _, slot_row = lax.sort_key_val(order, jnp.arange(n_assign, dtype=i32))
    bounds = jnp.searchsorted(sorted_e, jnp.arange(N_EXPERTS + 1, dtype=i32), side='left').astype(i32)
    start, count = bounds[:-1], bounds[1:] - bounds[:-1]
    n_blocks = -(-n_assign // MOE_ROWS)
    n_rows = n_blocks * MOE_ROWS
    first_blk = start // MOE_ROWS
    n_blk = jnp.where(count > 0, (start + count - 1) // MOE_ROWS - first_blk + 1, 0)
    pair_end = jnp.cumsum(n_blk)
    pair_start = pair_end - n_blk
    p = jnp.arange(n_blocks + N_EXPERTS - 1, dtype=i32)
    e_p = jnp.minimum(jnp.searchsorted(pair_end, p, side='right'), N_EXPERTS - 1).astype(i32)
    valid = p < pair_end[-1]
    blk_p = jnp.where(valid, first_blk[e_p] + p - pair_start[e_p], n_blocks - 1).astype(i32)
    base = blk_p * MOE_ROWS
    lo_p = jnp.where(valid, jnp.maximum(start[e_p], base) - base, 0).astype(i32)
    hi_p = jnp.where(valid, jnp.minimum(start[e_p] + count[e_p], base + MOE_ROWS) - base, 0).astype(i32)
    first_p = jnp.concatenate([jnp.ones((1,), i32), (blk_p[1:] != blk_p[:-1]).astype(i32)])
    d = xb_parts[0].shape[1]
    x_pad = jnp.concatenate(list(xb_parts) + [jnp.zeros((1, d), BF16)], 0)
    row_tok = jnp.concatenate([order // TOP_K, jnp.full((n_rows - n_assign,), n_tok, i32)])
    y_sorted = _experts(x_pad[row_tok], (e_p, blk_p, lo_p, hi_p, first_p), w_gu, w_down)
    outs, t0 = [], 0
    for part in xb_parts:
        t1 = t0 + part.shape[0]
        rows = slot_row[t0 * TOP_K:t1 * TOP_K]
        y = y_sorted[rows].astype(F32).reshape(t1 - t0, TOP_K, d)
        outs.append(jnp.sum(y * gate[t0:t1, :, None], axis=1))
        t0 = t1
    return outs


def _l2norm(x):
    return x * lax.rsqrt(jnp.sum(x * x, -1, keepdims=True) + 1e-6)


def _paged_body(pt_ref, q_ref, kn_ref, vn_ref, mask_ref, *refs, n_pg, page):
    del pt_ref
    k_refs, v_refs = refs[:n_pg], refs[n_pg:2 * n_pg]
    o_ref, m_ref, l_ref, acc_ref = refs[2 * n_pg:]
    j = pl.program_id(1)
    nt = (((1,), (1,)), ((), ()))
    hr = q_ref.shape[1] // DA_HEADS
    head = lambda h: slice(h * hr, (h + 1) * hr)
    q_heads = [q_ref[0, head(h), :].astype(BF16) for h in range(DA_HEADS)]

    @pl.when(j == 0)
    def _():
        m_ref[...] = jnp.full(m_ref.shape, NEG, F32)
        l_ref[...] = jnp.zeros(l_ref.shape, F32)
        acc_ref[...] = jnp.zeros(acc_ref.shape, F32)

    def scores(keys):
        return jnp.concatenate([lax.dot_general(q_heads[h], keys[h], nt, preferred_element_type=F32)
                                for h in range(DA_HEADS)], axis=0)

    def update(s, values):
        m_prev = m_ref[...]
        m_new = jnp.maximum(m_prev, jnp.max(s, axis=-1, keepdims=True))
        corr = jnp.exp(m_prev - m_new)
        p = jnp.exp(s - m_new)
        l_ref[...] = l_ref[...] * corr + jnp.sum(p, axis=-1, keepdims=True)
        acc = acc_ref[...] * corr
        new = []
        for h in range(DA_HEADS):
            a = acc[head(h)]
            for i, vals in enumerate(values):
                a = a + jnp.dot(p[head(h), i * page:(i + 1) * page].astype(BF16), vals[h],
                                preferred_element_type=F32)
            new.append(a)
        acc_ref[...] = jnp.concatenate(new, axis=0)
        m_ref[...] = m_new

    def head_rows(ref, h):
        return ref[0, pl.ds(h, page, stride=DA_HEADS), :].astype(BF16)

    s = [scores([head_rows(r, h) for h in range(DA_HEADS)]) for r in k_refs]
    update(s[0] if n_pg == 1 else jnp.concatenate(s, axis=1),
           [[head_rows(r, h) for h in range(DA_HEADS)] for r in v_refs])

    @pl.when(j == pl.num_programs(1) - 1)
    def _():
        lanes = lambda h: slice(h * DA_DV, (h + 1) * DA_DV)
        kn, vn = kn_ref[0], vn_ref[0]
        s_new = scores([kn[:, lanes(h)] for h in range(DA_HEADS)])
        update(jnp.where(mask_ref[...] > 0.5, s_new, NEG), [[vn[:, lanes(h)] for h in range(DA_HEADS)]])
        o_ref[0] = acc_ref[...] / l_ref[...]


def _diff_attn_sample(qb, kb, vb, lam, cache_k, cache_v, page_table):
    B, T, W = qb.shape
    n_pool, page = cache_k.shape[:2]
    n_pages = page_table.shape[1]
    n_pg = max(c for c in (4, 2, 1) if n_pages % c == 0)
    rows = DA_HEADS * 2 * T
    q4 = jnp.transpose(qb.astype(F32).reshape(B, T, DA_HEADS, DA_QK), (0, 2, 1, 3))
    comp = (jnp.arange(DA_QK) // DA_DH)[None, :] == jnp.arange(2)[:, None]
    q_rows = jnp.where(comp[None, None, :, None, :], q4[:, :, None], 0.0).reshape(B, rows, DA_QK)
    pad = lambda a: jnp.pad(a, ((0, 0), (0, page - T), (0, 0)))
    mask = (jnp.arange(page)[None, :] <= (jnp.arange(rows) % T)[:, None]).astype(F32)
    ck = cache_k.reshape(n_pool, page * DA_HEADS, DA_QK)
    cv = cache_v.reshape(n_pool, page * DA_HEADS, DA_DV)
    per_seq = lambda shape: pl.BlockSpec(shape, lambda b, j, pt: (b, 0, 0))
    page_spec = lambda i: pl.BlockSpec((1, page * DA_HEADS, DA_QK),
                                       lambda b, j, pt: (pt[b * n_pages + j * n_pg + i], 0, 0))
    grid_spec = pltpu.PrefetchScalarGridSpec(
        num_scalar_prefetch=1,
        grid=(B, n_pages // n_pg),
        in_specs=[per_seq((1, rows, DA_QK)), per_seq((1, page, W)), per_seq((1, page, W)),
                  pl.BlockSpec((rows, page), lambda b, j, pt: (0, 0))]
                 + [page_spec(i) for i in range(n_pg)] * 2,
        out_specs=per_seq((1, rows, DA_DV)),
        scratch_shapes=[pltpu.VMEM((rows, 1), F32), pltpu.VMEM((rows, 1), F32), pltpu.VMEM((rows, DA_DV), F32)],
    )
    o = pl.pallas_call(
        functools.partial(_paged_body, n_pg=n_pg, page=page),
        grid_spec=grid_spec,
        out_shape=jax.ShapeDtypeStruct((B, rows, DA_DV), F32),
        compiler_params=pltpu.CompilerParams(dimension_semantics=("parallel", "arbitrary"),
                                             vmem_limit_bytes=V7X_VMEM_LIMIT_BYTES),
        name="diff_attn_paged",
    )(page_table.reshape(-1), q_rows, pad(kb), pad(vb), mask, *([ck] * n_pg), *([cv] * n_pg))
    o = o.reshape(B, DA_HEADS, 2, T, DA_DV)
    o = o[:, :, 0] - lam * o[:, :, 1]
    return jnp.transpose(o, (0, 2, 1, 3))


def _gated_delta(q, k, v, beta, g, state0):
    B, T, H, DK = q.shape
    C = DN_CHUNK if T % DN_CHUNK == 0 else T
    NC = T // C

    def chunks(a):
        a = a.reshape((B, NC, C, H) + a.shape[3:])
        return jnp.moveaxis(a, 3, 2)

    q, k, v, beta, g = chunks(q), chunks(k), chunks(v), chunks(beta), chunks(g)
    G = jnp.cumsum(g, -1)
    incl = jnp.tril(jnp.ones((C, C), bool))
    strict = jnp.tril(jnp.ones((C, C), bool), -1)
    D = jnp.exp(jnp.where(incl, G[..., :, None] - G[..., None, :], -jnp.inf))
    kk = jnp.einsum('...id,...jd->...ij', k, k)
    L = jnp.where(strict, beta[..., :, None] * kk * D, 0.0) + jnp.eye(C, dtype=F32)
    W = lax.linalg.triangular_solve(L, beta[..., None] * v, left_side=True, lower=True, unit_diagonal=True)
    Y = lax.linalg.triangular_solve(L, (beta * jnp.exp(G))[..., None] * k, left_side=True, lower=True,
                                    unit_diagonal=True)
    P = jnp.einsum('...id,...jd->...ij', q, k) * D
    qg = q * jnp.exp(G)[..., None]
    kg = k * jnp.exp(G[..., -1:] - G)[..., None]
    glast = jnp.exp(G[..., -1])

    def step(S, xs):
        Wc, Yc, Pc, qgc, kgc, glc = xs
        U = Wc - Yc @ S
        O = qgc @ S + Pc @ U
        S = glc[..., None, None] * S + jnp.swapaxes(kgc, -1, -2) @ U
        return S, O

    xs = tuple(jnp.moveaxis(a, 1, 0) for a in (W, Y, P, qg, kg, glast))
    S, O = lax.scan(step, state0, xs)
    O = jnp.transpose(O, (1, 0, 3, 2, 4)).reshape(B, T, H, DK)
    return O, S


def _gdn_body(q_ref, k_ref, v_ref, col_ref, row_ref, s0_ref, o_ref, sf_ref, s_scr, *, n_chunks):
    j = pl.program_id(1)
    c_len = DN_CHUNK
    nn, nt, tn = ((2,), (1,)), ((2,), (2,)), ((1,), (1,))

    @pl.when(j == 0)
    def _():
        s_scr[...] = s0_ref[0]

    ii = lax.broadcasted_iota(jnp.int32, (1, c_len, c_len), 1)
    jj = lax.broadcasted_iota(jnp.int32, (1, c_len, c_len), 2)
    incl, strict = ii >= jj, ii > jj
    eye = jnp.where(ii == jj, 1.0, 0.0)
    bmm = lambda a, b, dims: lax.dot_general(a.astype(BF16), b.astype(BF16), (dims, ((0,), (0,))),
                                             preferred_element_type=F32)
    items = [(c, h) for c in range(n_chunks) for h in range(DN_HEADS)]
    rows = lambda c: slice(c * c_len, (c + 1) * c_len)
    lanes = lambda h: slice(h * DN_DH, (h + 1) * DN_DH)
    q = jnp.stack([q_ref[0, rows(c), lanes(h)] for c, h in items])
    k = jnp.stack([k_ref[0, rows(c), lanes(h)] for c, h in items])
    v = jnp.stack([v_ref[0, rows(c), lanes(h)] for c, h in items])
    beta = jnp.stack([col_ref[0, rows(c), h:h + 1] for c, h in items])
    g_col = jnp.stack([col_ref[0, rows(c), DN_HEADS + h:DN_HEADS + h + 1] for c, h in items])
    g_row = jnp.stack([row_ref[0, c, h:h + 1, :] for c, h in items])
    g_last = g_row[:, :, c_len - 1:c_len]
    decay = jnp.where(incl, jnp.exp(jnp.where(incl, g_col - g_row, 0.0)), 0.0)
    a = jnp.where(strict, beta * bmm(k, k, nt) * decay, 0.0)
    inv = eye - a
    power = a
    for _ in range(5):
        power = bmm(power, power, nn)
        inv = inv + bmm(inv, power, nn)
    e_g = jnp.exp(g_col)
    wy = bmm(inv, jnp.concatenate([beta * v, (beta * e_g) * k], axis=2), nn)
    w, y = wy[:, :, :DN_DH], wy[:, :, DN_DH:]
    p = bmm(q, k, nt) * decay
    qg = q * e_g
    kg = k * jnp.exp(g_last - g_col)
    g_tile = jnp.exp(g_last)
    state = s_scr[...]
    for c in range(n_chunks):
        sl = slice(c * DN_HEADS, (c + 1) * DN_HEADS)
        u = w[sl] - bmm(y[sl], state, nn)
        o = bmm(qg[sl], state, nn) + bmm(p[sl], u, nn)
        state = g_tile[sl] * state + bmm(kg[sl], u, tn)
        for h in range(DN_HEADS):
            o_ref[0, rows(c), lanes(h)] = o[h]
    s_scr[...] = state

    @pl.when(j == pl.num_programs(1) - 1)
    def _():
        sf_ref[0] = s_scr[...]


def _gated_delta_chunked(q, k, v, beta, g, state0):
    B, T, W = q.shape
    n_chunks = max(c for c in (4, 2, 1) if (T // DN_CHUNK) % c == 0)
    tile = n_chunks * DN_CHUNK
    G = jnp.cumsum(g.reshape(B, T // DN_CHUNK, DN_CHUNK, DN_HEADS), axis=2)
    col = jnp.concatenate([beta, G.reshape(B, T, DN_HEADS), jnp.zeros((B, T, LANES - 2 * DN_HEADS), F32)], -1)
    row = jnp.pad(jnp.swapaxes(G, 2, 3), ((0, 0), (0, 0), (0, 8 - DN_HEADS), (0, 0)))
    seq = lambda w: pl.BlockSpec((1, tile, w), lambda b, j: (b, j, 0))
    st = pl.BlockSpec((1, DN_HEADS, DN_DH, DN_DH), lambda b, j: (b, 0, 0, 0))
    return pl.pallas_call(
        functools.partial(_gdn_body, n_chunks=n_chunks),
        grid=(B, T // tile),
        in_specs=[seq(W), seq(W), seq(W), seq(LANES),
                  pl.BlockSpec((1, n_chunks, 8, DN_CHUNK), lambda b, j: (b, j, 0, 0)), st],
        out_specs=[seq(W), st],
        out_shape=[jax.ShapeDtypeStruct((B, T, W), F32),
                   jax.ShapeDtypeStruct((B, DN_HEADS, DN_DH, DN_DH), F32)],
        scratch_shapes=[pltpu.VMEM((DN_HEADS, DN_DH, DN_DH), F32)],
        compiler_params=pltpu.CompilerParams(dimension_semantics=("parallel", "arbitrary"),
                                             vmem_limit_bytes=V7X_VMEM_LIMIT_BYTES),
        name="gated_delta_chunked",
    )(q, k, v, col, row, state0)


def _delta_group(dn_qkv, ab, z, conv0, state0, lw):
    B, T, _ = dn_qkv.shape
    xp = jnp.concatenate([conv0, dn_qkv], axis=1)
    cw = lw['dn_conv_w']
    conv = xp[:, 0:T] * cw[0]
    for i in range(1, CONV_W):
        conv = conv + xp[:, i:i + T] * cw[i]
    conv = _silu(conv)
    new_conv = xp[:, T:]
    dq, dk, dv = jnp.split(conv, 3, axis=-1)
    dq = _l2norm(dq.reshape(B, T, DN_HEADS, DN_DH)) * DN_DH ** -0.5
    dk = _l2norm(dk.reshape(B, T, DN_HEADS, DN_DH))
    dv = dv.reshape(B, T, DN_HEADS, DN_DH)
    dn_a, dn_b = ab[..., :DN_HEADS], ab[..., DN_HEADS:IN_AB]
    beta = jax.nn.sigmoid(dn_b)
    g = -jnp.exp(lw['dn_a_log']) * jax.nn.softplus(dn_a + lw['dn_dt_bias'])
    if T % DN_CHUNK == 0:
        o_dn, new_state = _gated_delta_chunked(dq.reshape(B, T, DN_WIDTH), dk.reshape(B, T, DN_WIDTH),
                                               dv.reshape(B, T, DN_WIDTH), beta, g, state0)
        o_dn = o_dn.reshape(B, T, DN_HEADS, DN_DH)
    else:
        o_dn, new_state = _gated_delta(dq, dk, dv, beta, g, state0)
    zz = z.reshape(B, T, DN_HEADS, DN_DH)
    o_dn = o_dn * lax.rsqrt(jnp.mean(o_dn * o_dn, -1, keepdims=True) + 1e-6) * lw['dn_norm_w'] * _silu(zz)
    return o_dn.astype(BF16).reshape(B * T, DN_WIDTH), new_state, new_conv


def _cross_attn(q, mem_k, mem_v):
    B, T, _ = q.shape
    q = q.reshape(B, T, MEM_HEADS, MEM_DH)
    s = jnp.einsum('bthd,bmhd->bhtm', q, mem_k, preferred_element_type=F32) * MEM_DH ** -0.5
    p = jax.nn.softmax(s, axis=-1)
    o = jnp.einsum('bhtm,bmhd->bthd', p, mem_v)
    return o.astype(BF16).reshape(B * T, MEM_WIDTH)


def _mixer_and_cross(x, group, lw, lam, lam_init, alpha):
    B, T, D = x.shape
    n = B * T
    tm = _row_tile(T if group['prompt'] else n)
    qb, kf, vf, kb, vb, dn, z, ab = _in_proj(x.reshape(n, D), lw['w_in'], group['rope'],
                                             (T // tm) if group['prompt'] else 1, tm)
    if group['prompt']:
        o_da = _diff_attn_prompt(qb.reshape(B, T, DA_WIDTH), kb.reshape(B, T, DA_WIDTH),
                                 vb.reshape(B, T, DA_WIDTH), lam, lw['da_norm_w'], 1.0 - lam_init)
        o_da = o_da.reshape(n, DA_WIDTH)
    else:
        o = _diff_attn_sample(qb.reshape(B, T, DA_WIDTH), kb.reshape(B, T, DA_WIDTH), vb.reshape(B, T, DA_WIDTH),
                              lam, group['cache_k'], group['cache_v'], group['page_table'])
        o = o * lax.rsqrt(jnp.mean(o * o, -1, keepdims=True) + LN_EPS) * lw['da_norm_w'] * (1.0 - lam_init)
        o_da = o.astype(BF16).reshape(n, DA_WIDTH)
    o_dn, state, conv = _delta_group(dn.reshape(B, T, 3 * DN_WIDTH), ab.reshape(B, T, LANES),
                                     z.reshape(B, T, DN_WIDTH), group['conv0'], group['state0'], lw)
    x1, x1b = _mm_res_ln([o_da, o_dn], [lw['w_out'][:DA_WIDTH], lw['w_out'][DA_WIDTH:]],
                         x.reshape(n, D), lw['ln1_g'], lw['ln1_b'], alpha)
    qc = _matmul(x1b, lw['w_cq'])
    oc = _cross_attn(qc.reshape(B, T, MEM_WIDTH), group['mem_k'], group['mem_v'])
    x2, x2b, logits = _mm_res_ln([oc], [lw['w_co']], x1, lw['ln2_g'], lw['ln2_b'], alpha, lw['w_router'])
    return (x2, x2b, logits, kf.reshape(B, T, DA_HEADS, DA_QK), vf.reshape(B, T, DA_HEADS, DA_DV), state, conv)


def kernel(x_prompt, x_sample, cache_attn_k, cache_attn_v, cache_mem_k, cache_mem_v, state_dn, state_dn_conv,
           page_table, mem_prompt, w_in, da_lambda_q1, da_lambda_k1, da_lambda_q2, da_lambda_k2, da_norm_w,
           dn_conv_w, dn_a_log, dn_dt_bias, dn_norm_w, w_out, ln1_g, ln1_b, w_cq, w_ckv, w_co, ln2_g, ln2_b,
           w_router, router_bias, w_exp_gu, w_exp_down, w_sh_gu, w_sh_down, ln3_g, ln3_b):
    depth = w_in.shape[0]
    alpha = (2 * depth) ** 0.25
    Bp, Sp, D = x_prompt.shape
    Bs, Ts, _ = x_sample.shape
    past = page_table.shape[1] * cache_attn_k.shape[2]
    rope_p = _rope_tables(jnp.arange(Sp, dtype=jnp.int32))
    rope_s = _rope_tables(jnp.tile(past + jnp.arange(Ts, dtype=jnp.int32), Bs))
    yp, ys = x_prompt, x_sample
    outs = [[] for _ in range(10)]
    for l in range(depth):
        wi = w_in[l]
        w_in_perm = jnp.concatenate(
            [wi[:, :IN_MAIN], wi[:, IN_MAIN + IN_AB:], wi[:, IN_MAIN:IN_MAIN + IN_AB],
             jnp.zeros((D, LANES - IN_AB), wi.dtype)], axis=1).astype(BF16)
        lw = {'w_in': w_in_perm, 'da_norm_w': da_norm_w[l], 'dn_conv_w': dn_conv_w[l], 'dn_a_log': dn_a_log[l],
              'dn_dt_bias': dn_dt_bias[l], 'dn_norm_w': dn_norm_w[l], 'w_out': w_out[l].astype(BF16),
              'ln1_g': ln1_g[l], 'ln1_b': ln1_b[l], 'w_cq': w_cq[l].astype(BF16), 'w_co': w_co[l].astype(BF16),
              'ln2_g': ln2_g[l], 'ln2_b': ln2_b[l], 'w_router': w_router[l].astype(BF16)}
        lam_init = 0.8 - 0.6 * math.exp(-0.3 * l)
        lam = (jnp.exp(jnp.sum(da_lambda_q1[l] * da_lambda_k1[l]))
               - jnp.exp(jnp.sum(da_lambda_q2[l] * da_lambda_k2[l])) + lam_init)
        mkv = _matmul(mem_prompt.reshape(-1, D), w_ckv[l].astype(BF16))
        n_mem = mem_prompt.shape[1]
        mk_p = mkv[:, :MEM_WIDTH].reshape(Bp, n_mem, MEM_HEADS, MEM_DH)
        mv_p = mkv[:, MEM_WIDTH:].reshape(Bp, n_mem, MEM_HEADS, MEM_DH)
        grp_p = {'prompt': True, 'rope': rope_p, 'mem_k': mk_p, 'mem_v': mv_p,
                 'conv0': jnp.zeros((Bp, CONV_W - 1, 3 * DN_WIDTH), F32),
                 'state0': jnp.zeros((Bp, DN_HEADS, DN_DH, DN_DH), F32)}
        grp_s = {'prompt': False, 'rope': rope_s, 'mem_k': cache_mem_k[l], 'mem_v': cache_mem_v[l],
                 'conv0': state_dn_conv[l], 'state0': state_dn[l], 'cache_k': cache_attn_k[l],
                 'cache_v': cache_attn_v[l], 'page_table': page_table}
        x2p, x2bp, lgp, kp, vp, sp, cp = _mixer_and_cross(yp, grp_p, lw, lam, lam_init, alpha)
        x2s, x2bs, lgs, ks_, vs_, ss_, cs_ = _mixer_and_cross(ys, grp_s, lw, lam, lam_init, alpha)
        idx, gate = _route(jnp.concatenate([lgp, lgs], 0), router_bias[l])
        routed_p, routed_s = _routed([x2bp, x2bs], idx, gate, w_exp_gu[l].astype(BF16), w_exp_down[l].astype(BF16))
        wsg, wsd = w_sh_gu[l].astype(BF16), w_sh_down[l].astype(BF16)
        yp = _moe_out(x2p, routed_p, wsg, wsd, ln3_g[l], ln3_b[l], alpha).reshape(Bp, Sp, D)
        ys = _moe_out(x2s, routed_s, wsg, wsd, ln3_g[l], ln3_b[l], alpha).reshape(Bs, Ts, D)
        for lst, val in zip(outs, (kp, vp, ks_, vs_, sp, ss_, cp, cs_, mk_p, mv_p)):
            lst.append(val)
    return (yp, ys) + tuple(jnp.stack(o) for o in outs)
```

```python
import functools
import math

import jax
import jax.numpy as jnp
import numpy as np
from jax import lax
from jax.experimental import pallas as pl
from jax.experimental.pallas import tpu as pltpu

D_MODEL = 1024
DA_HEADS = 4
DA_DH = 64
DA_QK = 2 * DA_DH
DA_DV = 2 * DA_DH
DA_WIDTH = DA_HEADS * DA_DV
ROT_DIM = DA_DH // 4
ROPE_THETA = 500000.0
DN_HEADS = 4
DN_DH = 128
DN_WIDTH = DN_HEADS * DN_DH
CONV_W = 4
DN_CHUNK = 64
MEM_HEADS = 4
MEM_DH = 128
MEM_WIDTH = MEM_HEADS * MEM_DH
N_EXPERTS = 256
TOP_K = 8
N_GROUPS = 8
TOPK_GROUPS = 4
D_EXPERT = 256
ROUTED_SCALE = 2.5
LN_EPS = 1e-5
NEG = -1e30

LANES = 128
V7X_VMEM_LIMIT_BYTES = 48 * 1024 * 1024
ROW_TILE = 512
ATTN_TILE = 512
MOE_ROWS = 512
IN_MAIN = 3 * DA_WIDTH + 3 * DN_WIDTH
IN_AB = 2 * DN_HEADS
IN_PERM_COLS = IN_MAIN + DN_WIDTH + LANES

BF16 = jnp.bfloat16
F32 = jnp.float32


def _params(n_axes=1):
    return pltpu.CompilerParams(dimension_semantics=("parallel",) * n_axes,
                                vmem_limit_bytes=V7X_VMEM_LIMIT_BYTES)


def _row_tile(n):
    return ROW_TILE if n % ROW_TILE == 0 else n


def _in_proj_body(x_ref, w_ref, c_ref, sa_ref, sb_ref,
                  q_ref, kf_ref, vf_ref, kb_ref, vb_ref, dn_ref, z_ref, ab_ref):
    h = jnp.dot(x_ref[...].astype(BF16), w_ref[...], preferred_element_type=F32)
    reps = DA_WIDTH // LANES
    c = jnp.concatenate([c_ref[...]] * reps, axis=1)
    sa = jnp.concatenate([sa_ref[...]] * reps, axis=1)
    sb = jnp.concatenate([sb_ref[...]] * reps, axis=1)

    def rope(t):
        return (t * c + pltpu.roll(t, DA_WIDTH - ROT_DIM // 2, 1) * sa
                + pltpu.roll(t, ROT_DIM // 2, 1) * sb)

    q = rope(h[:, 0:DA_WIDTH])
    k = rope(h[:, DA_WIDTH:2 * DA_WIDTH])
    v = h[:, 2 * DA_WIDTH:3 * DA_WIDTH]
    q_ref[...] = (q * DA_DH ** -0.5).astype(BF16)
    kf_ref[...] = k
    vf_ref[...] = v
    kb_ref[...] = k.astype(BF16)
    vb_ref[...] = v.astype(BF16)
    dn_ref[...] = h[:, 3 * DA_WIDTH:IN_MAIN]
    z_ref[...] = h[:, IN_MAIN:IN_MAIN + DN_WIDTH]
    ab_ref[...] = h[:, IN_MAIN + DN_WIDTH:]


def _in_proj(x, w_perm, tables, n_pos_blocks, tm):
    n = x.shape[0]
    row = lambda w: pl.BlockSpec((tm, w), lambda i: (i, 0))
    tab = pl.BlockSpec((tm, LANES), lambda i: (i % n_pos_blocks, 0))
    widths = (DA_WIDTH, DA_WIDTH, DA_WIDTH, DA_WIDTH, DA_WIDTH, 3 * DN_WIDTH, DN_WIDTH, LANES)
    dtypes = (BF16, F32, F32, BF16, BF16, F32, F32, F32)
    return pl.pallas_call(
        _in_proj_body,
        grid=(n // tm,),
        in_specs=[row(D_MODEL), pl.BlockSpec((D_MODEL, IN_PERM_COLS), lambda i: (0, 0)), tab, tab, tab],
        out_specs=[row(w) for w in widths],
        out_shape=[jax.ShapeDtypeStruct((n, w), d) for w, d in zip(widths, dtypes)],
        compiler_params=_params(),
        name="in_proj_rope",
    )(x, w_perm, *tables)


def _rope_tables(pos):
    half = ROT_DIM // 2
    inv = ROPE_THETA ** (-jnp.arange(0, ROT_DIM, 2, dtype=F32) / ROT_DIM)
    ang = pos.astype(F32)[:, None] * inv
    cos, sin = jnp.cos(ang), jnp.sin(ang)
    n = pos.shape[0]
    ones = jnp.ones((n, DA_DH - ROT_DIM), F32)
    zeros = jnp.zeros((n, DA_DH - ROT_DIM), F32)
    zh = jnp.zeros((n, half), F32)
    c = jnp.concatenate([cos, cos, ones], 1)
    sa = jnp.concatenate([-sin, zh, zeros], 1)
    sb = jnp.concatenate([zh, sin, zeros], 1)
    return tuple(jnp.concatenate([t, t], 1) for t in (c, sa, sb))


def _flash_body(lam_ref, q_ref, k_ref, v_ref, nw_ref, o_ref, *, t, out_scale):
    qi = pl.program_id(2)
    q = q_ref[0]
    lane = lax.broadcasted_iota(jnp.int32, q.shape, 1)
    zero = jnp.zeros_like(q)
    qq = jnp.concatenate([jnp.where(lane < DA_DH, q, zero), jnp.where(lane >= DA_DH, q, zero)], axis=0)

    def update(j, carry, masked):
        m, l, acc = carry
        start = pl.multiple_of(j * t, t)
        kb = k_ref[0, pl.ds(start, t), :]
        vb = v_ref[0, pl.ds(start, t), :]
        s = lax.dot_general(qq, kb, (((1,), (1,)), ((), ())), preferred_element_type=F32)
        if masked:
            r = lax.broadcasted_iota(jnp.int32, s.shape, 0)
            c = lax.broadcasted_iota(jnp.int32, s.shape, 1)
            r = jnp.where(r >= t, r - t, r)
            s = jnp.where(c <= r, s, NEG)
        m_new = jnp.maximum(m, jnp.max(s, axis=-1, keepdims=True))
        corr = jnp.exp(m - m_new)
        p = jnp.exp(s - m_new)
        l = l * corr + jnp.sum(p, axis=-1, keepdims=True)
        acc = acc * corr + jnp.dot(p.astype(BF16), vb, preferred_element_type=F32)
        return m_new, l, acc

    init = (jnp.full((2 * t, 1), NEG, F32), jnp.zeros((2 * t, 1), F32), jnp.zeros((2 * t, DA_DV), F32))
    carry = lax.fori_loop(0, qi, lambda j, c: update(j, c, False), init)
    m, l, acc = update(qi, carry, True)
    o = acc / l
    o = o[:t] - lam_ref[0] * o[t:]
    o = o * lax.rsqrt(jnp.mean(o * o, axis=-1, keepdims=True) + LN_EPS) * nw_ref[...] * out_scale
    o_ref[0] = o.astype(o_ref.dtype)


def _diff_attn_prompt(qb, kb, vb, lam, norm_w, out_scale):
    b, s, _ = qb.shape
    t = min(ATTN_TILE, s)
    return pl.pallas_call(
        functools.partial(_flash_body, t=t, out_scale=out_scale),
        grid=(b, DA_HEADS, s // t),
        in_specs=[pl.BlockSpec(memory_space=pltpu.SMEM),
                  pl.BlockSpec((1, t, DA_QK), lambda bi, h, i: (bi, i, h)),
                  pl.BlockSpec((1, s, DA_QK), lambda bi, h, i: (bi, 0, h)),
                  pl.BlockSpec((1, s, DA_DV), lambda bi, h, i: (bi, 0, h)),
                  pl.BlockSpec((1, DA_DV), lambda bi, h, i: (0, 0))],
        out_specs=pl.BlockSpec((1, t, DA_DV), lambda bi, h, i: (bi, i, h)),
        out_shape=jax.ShapeDtypeStruct((b, s, DA_WIDTH), BF16),
        compiler_params=_params(3),
        name="diff_attn_prompt",
    )(lam.reshape(1), qb, kb, vb, norm_w.reshape(1, DA_DV))


def _mm_body(x_ref, w_ref, o_ref):
    o_ref[...] = jnp.dot(x_ref[...].astype(BF16), w_ref[...], preferred_element_type=F32).astype(o_ref.dtype)


def _matmul(x, w, out_dtype=F32):
    m, k = x.shape
    n = w.shape[1]
    tm = _row_tile(m)
    return pl.pallas_call(
        _mm_body,
        grid=(m // tm,),
        in_specs=[pl.BlockSpec((tm, k), lambda i: (i, 0)), pl.BlockSpec((k, n), lambda i: (0, 0))],
        out_specs=pl.BlockSpec((tm, n), lambda i: (i, 0)),
        out_shape=jax.ShapeDtypeStruct((m, n), out_dtype),
        compiler_params=_params(),
        name="dense_matmul",
    )(x, w)


def _ln(x, g, b):
    mu = jnp.mean(x, axis=-1, keepdims=True)
    d = x - mu
    var = jnp.mean(d * d, axis=-1, keepdims=True)
    return d * lax.rsqrt(var + LN_EPS) * g + b


def _mm_res_ln_body(*refs, n_x, alpha, with_logits):
    xs = refs[:n_x]
    ws = refs[n_x:2 * n_x]
    res_ref, g_ref, b_ref = refs[2 * n_x:2 * n_x + 3]
    rest = refs[2 * n_x + 3:]
    h = alpha * res_ref[...]
    for x_ref, w_ref in zip(xs, ws):
        h = h + jnp.dot(x_ref[...].astype(BF16), w_ref[...], preferred_element_type=F32)
    y = _ln(h, g_ref[...], b_ref[...])
    if with_logits:
        wr_ref, y_ref, yb_ref, lg_ref = rest
    else:
        y_ref, yb_ref = rest
    yb = y.astype(BF16)
    y_ref[...] = y
    yb_ref[...] = yb
    if with_logits:
        lg_ref[...] = jnp.dot(yb, wr_ref[...], preferred_element_type=F32)


def _mm_res_ln(xs, ws, res, g, b, alpha, w_router=None):
    m, d = res.shape
    tm = _row_tile(m)
    row = lambda w: pl.BlockSpec((tm, w), lambda i: (i, 0))
    full = lambda a: pl.BlockSpec(a.shape, lambda i: (0, 0))
    g2, b2 = g.reshape(1, d), b.reshape(1, d)
    in_specs = [row(x.shape[1]) for x in xs] + [full(w) for w in ws] + [row(d), full(g2), full(b2)]
    args = list(xs) + list(ws) + [res, g2, b2]
    out_specs = [row(d), row(d)]
    out_shape = [jax.ShapeDtypeStruct((m, d), F32), jax.ShapeDtypeStruct((m, d), BF16)]
    if w_router is not None:
        in_specs.append(full(w_router))
        args.append(w_router)
        out_specs.append(row(w_router.shape[1]))
        out_shape.append(jax.ShapeDtypeStruct((m, w_router.shape[1]), F32))
    return pl.pallas_call(
        functools.partial(_mm_res_ln_body, n_x=len(xs), alpha=alpha, with_logits=w_router is not None),
        grid=(m // tm,),
        in_specs=in_specs, out_specs=out_specs, out_shape=out_shape,
        compiler_params=_params(),
        name="matmul_residual_layernorm",
    )(*args)


def _silu(x):
    return x * jax.nn.sigmoid(x)


def _moe_out_body(x_ref, r_ref, wgu_ref, wd_ref, g_ref, b_ref, y_ref, *, alpha):
    x = x_ref[...]
    h = jnp.dot(x.astype(BF16), wgu_ref[...], preferred_element_type=F32)
    d_sh = h.shape[1] // 2
    a = (_silu(h[:, :d_sh]) * h[:, d_sh:]).astype(BF16)
    shared = jnp.dot(a, wd_ref[...], preferred_element_type=F32)
    y_ref[...] = _ln(alpha * x + (r_ref[...] + shared), g_ref[...], b_ref[...])


def _moe_out(x, routed, w_sh_gu, w_sh_down, g, b, alpha):
    m, d = x.shape
    tm = _row_tile(m)
    row = pl.BlockSpec((tm, d), lambda i: (i, 0))
    full = lambda a: pl.BlockSpec(a.shape, lambda i: (0, 0))
    g2, b2 = g.reshape(1, d), b.reshape(1, d)
    return pl.pallas_call(
        functools.partial(_moe_out_body, alpha=alpha),
        grid=(m // tm,),
        in_specs=[row, row, full(w_sh_gu), full(w_sh_down), full(g2), full(b2)],
        out_specs=row,
        out_shape=jax.ShapeDtypeStruct((m, d), F32),
        compiler_params=_params(),
        name="shared_expert_layernorm",
    )(x, routed, w_sh_gu, w_sh_down, g2, b2)


def _route_body(lg_ref, bias_ref, idx_ref, gate_ref):
    scores = jax.nn.sigmoid(lg_ref[...])
    biased = scores + bias_ref[...]
    lane = lax.broadcasted_iota(jnp.int32, scores.shape, 1)
    lane_f = lane.astype(F32)
    grp = lane // (N_EXPERTS // N_GROUPS)
    neg = -jnp.inf
    big = float(N_EXPERTS)

    def first_argmax(x):
        m = jnp.max(x, axis=-1, keepdims=True)
        return m, jnp.min(jnp.where(x == m, lane_f, big), axis=-1, keepdims=True)

    group_score = []
    for g in range(N_GROUPS):
        xm = jnp.where(grp == g, biased, neg)
        m1, i1 = first_argmax(xm)
        m2 = jnp.max(jnp.where(lane_f == i1, neg, xm), axis=-1, keepdims=True)
        group_score.append(m1 + m2)
    masked = jnp.full(scores.shape, neg, F32)
    for g in range(N_GROUPS):
        rank = jnp.zeros_like(group_score[g])
        for h in range(N_GROUPS):
            if h != g:
                beats = (group_score[h] >= group_score[g]) if h < g else (group_score[h] > group_score[g])
                rank = rank + jnp.where(beats, 1.0, 0.0)
        masked = jnp.where(grp == g, jnp.where(rank < TOPK_GROUPS, biased, neg), masked)
    out_lane = lax.broadcasted_iota(jnp.int32, idx_ref.shape, 1)
    idx_out = jnp.zeros(idx_ref.shape, F32)
    val_out = jnp.zeros(idx_ref.shape, F32)
    for k in range(TOP_K):
        _, i = first_argmax(masked)
        hit = lane_f == i
        v = jnp.sum(jnp.where(hit, scores, 0.0), axis=-1, keepdims=True)
        masked = jnp.where(hit, neg, masked)
        idx_out = jnp.where(out_lane == k, i, idx_out)
        val_out = jnp.where(out_lane == k, v, val_out)
    total = jnp.sum(val_out, axis=-1, keepdims=True)
    idx_ref[...] = idx_out.astype(jnp.int32)
    gate_ref[...] = val_out / total * ROUTED_SCALE


def _route(logits, router_bias):
    n = logits.shape[0]
    tm = _row_tile(n)
    row = lambda w: pl.BlockSpec((tm, w), lambda i: (i, 0))
    idx, gate = pl.pallas_call(
        _route_body,
        grid=(n // tm,),
        in_specs=[row(N_EXPERTS), pl.BlockSpec((1, N_EXPERTS), lambda i: (0, 0))],
        out_specs=[row(LANES), row(LANES)],
        out_shape=[jax.ShapeDtypeStruct((n, LANES), jnp.int32), jax.ShapeDtypeStruct((n, LANES), F32)],
        compiler_params=_params(),
        name="router_topk",
    )(logits, router_bias.reshape(1, N_EXPERTS))
    return idx[:, :TOP_K], gate[:, :TOP_K]


def _experts_body(e_ref, blk_ref, lo_ref, hi_ref, first_ref, x_ref, wgu_ref, wd_ref, o_ref):
    del e_ref, blk_ref
    p = pl.program_id(0)
    lo, hi = lo_ref[p], hi_ref[p]

    def expert_rows():
        h = jnp.dot(x_ref[...], wgu_ref[0], preferred_element_type=F32)
        a = (_silu(h[:, :D_EXPERT]) * h[:, D_EXPERT:]).astype(BF16)
        y = jnp.dot(a, wd_ref[0], preferred_element_type=F32).astype(o_ref.dtype)
        row = lax.broadcasted_iota(jnp.int32, y.shape, 0)
        return y, row

    @pl.when((hi > lo) & (first_ref[p] == 1))
    def _():
        y, row = expert_rows()
        zero = jnp.zeros_like(y)
        o_ref[...] = jnp.where(row >= lo, jnp.where(row < hi, y, zero), zero)

    @pl.when((hi > lo) & (first_ref[p] == 0))
    def _():
        y, row = expert_rows()
        prev = o_ref[...]
        o_ref[...] = jnp.where(row >= lo, jnp.where(row < hi, y, prev), prev)


def _experts(x_sorted, meta, w_gu, w_down):
    n_rows, d = x_sorted.shape
    n_pairs = meta[0].shape[0]
    grid_spec = pltpu.PrefetchScalarGridSpec(
        num_scalar_prefetch=5,
        grid=(n_pairs,),
        in_specs=[pl.BlockSpec((MOE_ROWS, d), lambda p, e, blk, lo, hi, first: (blk[p], 0)),
                  pl.BlockSpec((1, d, 2 * D_EXPERT), lambda p, e, blk, lo, hi, first: (e[p], 0, 0)),
                  pl.BlockSpec((1, D_EXPERT, d), lambda p, e, blk, lo, hi, first: (e[p], 0, 0))],
        out_specs=pl.BlockSpec((MOE_ROWS, d), lambda p, e, blk, lo, hi, first: (blk[p], 0)),
    )
    return pl.pallas_call(
        _experts_body,
        grid_spec=grid_spec,
        out_shape=jax.ShapeDtypeStruct((n_rows, d), BF16),
        compiler_params=pltpu.CompilerParams(dimension_semantics=("arbitrary",),
                                             vmem_limit_bytes=V7X_VMEM_LIMIT_BYTES),
        name="routed_experts",
    )(*meta, x_sorted, w_gu, w_down)


def _routed(xb_parts, idx, gate, w_gu, w_down):
    n_tok = idx.shape[0]
    n_assign = n_tok * TOP_K
    i32 = jnp.int32
    sorted_e, order = lax.sort_key_val(idx.reshape(-1), jnp.arange(n_assign, dtype=i32))
    _, slot_row = lax.sort_key_val(order, jnp.arange(n_assign, dtype=i32))
    bounds = jnp.searchsorted(sorted_e, jnp.arange(N_EXPERTS + 1, dtype=i32), side='left').astype(i32)
    start, count = bounds[:-1], bounds[1:] - bounds[:-1]
    n_blocks = -(-n_assign // MOE_ROWS)
    n_rows = n_blocks * MOE_ROWS
    first_blk = start // MOE_ROWS
    n_blk = jnp.where(count > 0, (start + count - 1) // MOE_ROWS - first_blk + 1, 0)
    pair_end = jnp.cumsum(n_blk)
    pair_start = pair_end - n_blk
    p = jnp.arange(n_blocks + N_EXPERTS - 1, dtype=i32)
    e_p = jnp.minimum(jnp.searchsorted(pair_end, p, side='right'), N_EXPERTS - 1).astype(i32)
    valid = p < pair_end[-1]
    blk_p = jnp.where(valid, first_blk[e_p] + p - pair_start[e_p], n_blocks - 1).astype(i32)
    base = blk_p * MOE_ROWS
    lo_p = jnp.where(valid, jnp.maximum(start[e_p], base) - base, 0).astype(i32)
    hi_p = jnp.where(valid, jnp.minimum(start[e_p] + count[e_p], base + MOE_ROWS) - base, 0).astype(i32)
    first_p = jnp.concatenate([jnp.ones((1,), i32), (blk_p[1:] != blk_p[:-1]).astype(i32)])
    d = xb_parts[0].shape[1]
    x_pad = jnp.concatenate(list(xb_parts) + [jnp.zeros((1, d), BF16)], 0)
    row_tok = jnp.concatenate([order // TOP_K, jnp.full((n_rows - n_assign,), n_tok, i32)])
    y_sorted = _experts(x_pad[row_tok], (e_p, blk_p, lo_p, hi_p, first_p), w_gu, w_down)
    outs, t0 = [], 0
    for part in xb_parts:
        t1 = t0 + part.shape[0]
        rows = slot_row[t0 * TOP_K:t1 * TOP_K]
        y = y_sorted[rows].astype(F32).reshape(t1 - t0, TOP_K, d)
        outs.append(jnp.sum(y * gate[t0:t1, :, None], axis=1))
        t0 = t1
    return outs


def _l2norm(x):
    return x * lax.rsqrt(jnp.sum(x * x, -1, keepdims=True) + 1e-6)


def _paged_body(pt_ref, q_ref, kn_ref, vn_ref, mask_ref, *refs, n_pg, page):
    del pt_ref
    k_refs, v_refs = refs[:n_pg], refs[n_pg:2 * n_pg]
    o_ref, m_ref, l_ref, acc_ref = refs[2 * n_pg:]
    j = pl.program_id(1)
    nt = (((1,), (1,)), ((), ()))
    hr = q_ref.shape[1] // DA_HEADS
    head = lambda h: slice(h * hr, (h + 1) * hr)
    q_heads = [q_ref[0, head(h), :].astype(BF16) for h in range(DA_HEADS)]

    @pl.when(j == 0)
    def _():
        m_ref[...] = jnp.full(m_ref.shape, NEG, F32)
        l_ref[...] = jnp.zeros(l_ref.shape, F32)
        acc_ref[...] = jnp.zeros(acc_ref.shape, F32)

    def scores(keys):
        return jnp.concatenate([lax.dot_general(q_heads[h], keys[h], nt, preferred_element_type=F32)
                                for h in range(DA_HEADS)], axis=0)

    def update(s, values):
        m_prev = m_ref[...]
        m_new = jnp.maximum(m_prev, jnp.max(s, axis=-1, keepdims=True))
        corr = jnp.exp(m_prev - m_new)
        p = jnp.exp(s - m_new)
        l_ref[...] = l_ref[...] * corr + jnp.sum(p, axis=-1, keepdims=True)
        acc = acc_ref[...] * corr
        new = []
        for h in range(DA_HEADS):
            a = acc[head(h)]
            for i, vals in enumerate(values):
                a = a + jnp.dot(p[head(h), i * page:(i + 1) * page].astype(BF16), vals[h],
                                preferred_element_type=F32)
            new.append(a)
        acc_ref[...] = jnp.concatenate(new, axis=0)
        m_ref[...] = m_new

    def head_rows(ref, h):
        return ref[0, pl.ds(h, page, stride=DA_HEADS), :].astype(BF16)

    s = [scores([head_rows(r, h) for h in range(DA_HEADS)]) for r in k_refs]
    update(s[0] if n_pg == 1 else jnp.concatenate(s, axis=1),
           [[head_rows(r, h) for h in range(DA_HEADS)] for r in v_refs])

    @pl.when(j == pl.num_programs(1) - 1)
    def _():
        lanes = lambda h: slice(h * DA_DV, (h + 1) * DA_DV)
        kn, vn = kn_ref[0], vn_ref[0]
        s_new = scores([kn[:, lanes(h)] for h in range(DA_HEADS)])
        update(jnp.where(mask_ref[...] > 0.5, s_new, NEG), [[vn[:, lanes(h)] for h in range(DA_HEADS)]])
        o_ref[0] = acc_ref[...] / l_ref[...]


def _diff_attn_sample(qb, kb, vb, lam, cache_k, cache_v, page_table):
    B, T, W = qb.shape
    n_pool, page = cache_k.shape[:2]
    n_pages = page_table.shape[1]
    n_pg = max(c for c in (8, 4, 2, 1) if n_pages % c == 0)
    rows = DA_HEADS * 2 * T
    q4 = jnp.transpose(qb.astype(F32).reshape(B, T, DA_HEADS, DA_QK), (0, 2, 1, 3))
    comp = (jnp.arange(DA_QK) // DA_DH)[None, :] == jnp.arange(2)[:, None]
    q_rows = jnp.where(comp[None, None, :, None, :], q4[:, :, None], 0.0).reshape(B, rows, DA_QK)
    pad = lambda a: jnp.pad(a, ((0, 0), (0, page - T), (0, 0)))
    mask = (jnp.arange(page)[None, :] <= (jnp.arange(rows) % T)[:, None]).astype(F32)
    ck = cache_k.reshape(n_pool, page * DA_HEADS, DA_QK)
    cv = cache_v.reshape(n_pool, page * DA_HEADS, DA_DV)
    per_seq = lambda shape: pl.BlockSpec(shape, lambda b, j, pt: (b, 0, 0))
    page_spec = lambda i: pl.BlockSpec((1, page * DA_HEADS, DA_QK),
                                       lambda b, j, pt: (pt[b * n_pages + j * n_pg + i], 0, 0))
    grid_spec = pltpu.PrefetchScalarGridSpec(
        num_scalar_prefetch=1,
        grid=(B, n_pages // n_pg),
        in_specs=[per_seq((1, rows, DA_QK)), per_seq((1, page, W)), per_seq((1, page, W)),
                  pl.BlockSpec((rows, page), lambda b, j, pt: (0, 0))]
                 + [page_spec(i) for i in range(n_pg)] * 2,
        out_specs=per_seq((1, rows, DA_DV)),
        scratch_shapes=[pltpu.VMEM((rows, 1), F32), pltpu.VMEM((rows, 1), F32), pltpu.VMEM((rows, DA_DV), F32)],
    )
    o = pl.pallas_call(
        functools.partial(_paged_body, n_pg=n_pg, page=page),
        grid_spec=grid_spec,
        out_shape=jax.ShapeDtypeStruct((B, rows, DA_DV), F32),
        compiler_params=pltpu.CompilerParams(dimension_semantics=("parallel", "arbitrary"),
                                             vmem_limit_bytes=V7X_VMEM_LIMIT_BYTES),
        name="diff_attn_paged",
    )(page_table.reshape(-1), q_rows, pad(kb), pad(vb), mask, *([ck] * n_pg), *([cv] * n_pg))
    o = o.reshape(B, DA_HEADS, 2, T, DA_DV)
    o = o[:, :, 0] - lam * o[:, :, 1]
    return jnp.transpose(o, (0, 2, 1, 3))


def _gated_delta(q, k, v, beta, g, state0):
    B, T, H, DK = q.shape
    C = DN_CHUNK if T % DN_CHUNK == 0 else T
    NC = T // C

    def chunks(a):
        a = a.reshape((B, NC, C, H) + a.shape[3:])
        return jnp.moveaxis(a, 3, 2)

    q, k, v, beta, g = chunks(q), chunks(k), chunks(v), chunks(beta), chunks(g)
    G = jnp.cumsum(g, -1)
    incl = jnp.tril(jnp.ones((C, C), bool))
    strict = jnp.tril(jnp.ones((C, C), bool), -1)
    D = jnp.exp(jnp.where(incl, G[..., :, None] - G[..., None, :], -jnp.inf))
    kk = jnp.einsum('...id,...jd->...ij', k, k)
    L = jnp.where(strict, beta[..., :, None] * kk * D, 0.0) + jnp.eye(C, dtype=F32)
    W = lax.linalg.triangular_solve(L, beta[..., None] * v, left_side=True, lower=True, unit_diagonal=True)
    Y = lax.linalg.triangular_solve(L, (beta * jnp.exp(G))[..., None] * k, left_side=True, lower=True,
                                    unit_diagonal=True)
    P = jnp.einsum('...id,...jd->...ij', q, k) * D
    qg = q * jnp.exp(G)[..., None]
    kg = k * jnp.exp(G[..., -1:] - G)[..., None]
    glast = jnp.exp(G[..., -1])

    def step(S, xs):
        Wc, Yc, Pc, qgc, kgc, glc = xs
        U = Wc - Yc @ S
        O = qgc @ S + Pc @ U
        S = glc[..., None, None] * S + jnp.swapaxes(kgc, -1, -2) @ U
        return S, O

    xs = tuple(jnp.moveaxis(a, 1, 0) for a in (W, Y, P, qg, kg, glast))
    S, O = lax.scan(step, state0, xs)
    O = jnp.transpose(O, (1, 0, 3, 2, 4)).reshape(B, T, H, DK)
    return O, S


def _gdn_body(q_ref, k_ref, v_ref, z_ref, col_ref, row_ref, nw_ref, s0_ref, o_ref, sf_ref, s_scr, *, n_chunks):
    j = pl.program_id(1)
    c_len = DN_CHUNK
    nn, nt, tn = ((2,), (1,)), ((2,), (2,)), ((1,), (1,))

    @pl.when(j == 0)
    def _():
        s_scr[...] = s0_ref[0]

    ii = lax.broadcasted_iota(jnp.int32, (1, c_len, c_len), 1)
    jj = lax.broadcasted_iota(jnp.int32, (1, c_len, c_len), 2)
    incl, strict = ii >= jj, ii > jj
    eye = jnp.where(ii == jj, 1.0, 0.0)
    bmm = lambda a, b, dims: lax.dot_general(a.astype(BF16), b.astype(BF16), (dims, ((0,), (0,))),
                                             preferred_element_type=F32)
    items = [(c, h) for c in range(n_chunks) for h in range(DN_HEADS)]
    rows = lambda c: slice(c * c_len, (c + 1) * c_len)
    lanes = lambda h: slice(h * DN_DH, (h + 1) * DN_DH)
    l2n = lambda x: x * lax.rsqrt(jnp.sum(x * x, axis=-1, keepdims=True) + 1e-6)
    q = l2n(jnp.stack([q_ref[0, rows(c), lanes(h)] for c, h in items])) * DN_DH ** -0.5
    k = l2n(jnp.stack([k_ref[0, rows(c), lanes(h)] for c, h in items]))
    v = jnp.stack([v_ref[0, rows(c), lanes(h)] for c, h in items])
    beta = jnp.stack([col_ref[0, rows(c), h:h + 1] for c, h in items])
    g_col = jnp.stack([col_ref[0, rows(c), DN_HEADS + h:DN_HEADS + h + 1] for c, h in items])
    g_row = jnp.stack([row_ref[0, c, h:h + 1, :] for c, h in items])
    g_last = g_row[:, :, c_len - 1:c_len]
    decay = jnp.where(incl, jnp.exp(jnp.where(incl, g_col - g_row, 0.0)), 0.0)
    a = jnp.where(strict, beta * bmm(k, k, nt) * decay, 0.0)
    inv = eye - a
    power = a
    for _ in range(5):
        power = bmm(power, power, nn)
        inv = inv + bmm(inv, power, nn)
    e_g = jnp.exp(g_col)
    wy = bmm(inv, jnp.concatenate([beta * v, (beta * e_g) * k], axis=2), nn)
    w, y = wy[:, :, :DN_DH], wy[:, :, DN_DH:]
    p = bmm(q, k, nt) * decay
    qg = q * e_g
    kg = k * jnp.exp(g_last - g_col)
    g_tile = jnp.exp(g_last)
    state = s_scr[...]
    for c in range(n_chunks):
        sl = slice(c * DN_HEADS, (c + 1) * DN_HEADS)
        u = w[sl] - bmm(y[sl], state, nn)
        o = bmm(qg[sl], state, nn) + bmm(p[sl], u, nn)
        state = g_tile[sl] * state + bmm(kg[sl], u, tn)
        o = o * lax.rsqrt(jnp.mean(o * o, axis=-1, keepdims=True) + 1e-6) * nw_ref[...]
        for h in range(DN_HEADS):
            o_ref[0, rows(c), lanes(h)] = (o[h] * _silu(z_ref[0, rows(c), lanes(h)])).astype(o_ref.dtype)
    s_scr[...] = state

    @pl.when(j == pl.num_programs(1) - 1)
    def _():
        sf_ref[0] = s_scr[...]


def _gated_delta_chunked(qkv, z, beta, g, norm_w, state0):
    B, T, W = z.shape
    n_chunks = max(c for c in (4, 2, 1) if (T // DN_CHUNK) % c == 0)
    tile = n_chunks * DN_CHUNK
    G = jnp.cumsum(g.reshape(B, T // DN_CHUNK, DN_CHUNK, DN_HEADS), axis=2)
    col = jnp.concatenate([beta, G.reshape(B, T, DN_HEADS), jnp.zeros((B, T, LANES - 2 * DN_HEADS), F32)], -1)
    row = jnp.pad(jnp.swapaxes(G, 2, 3), ((0, 0), (0, 0), (0, 8 - DN_HEADS), (0, 0)))
    seq = lambda w, part=0: pl.BlockSpec((1, tile, w), lambda b, j: (b, j, part))
    st = pl.BlockSpec((1, DN_HEADS, DN_DH, DN_DH), lambda b, j: (b, 0, 0, 0))
    return pl.pallas_call(
        functools.partial(_gdn_body, n_chunks=n_chunks),
        grid=(B, T // tile),
        in_specs=[seq(W, 0), seq(W, 1), seq(W, 2), seq(W), seq(LANES),
                  pl.BlockSpec((1, n_chunks, 8, DN_CHUNK), lambda b, j: (b, j, 0, 0)),
                  pl.BlockSpec((1, DN_DH), lambda b, j: (0, 0)), st],
        out_specs=[seq(W), st],
        out_shape=[jax.ShapeDtypeStruct((B, T, W), BF16),
                   jax.ShapeDtypeStruct((B, DN_HEADS, DN_DH, DN_DH), F32)],
        scratch_shapes=[pltpu.VMEM((DN_HEADS, DN_DH, DN_DH), F32)],
        compiler_params=pltpu.CompilerParams(dimension_semantics=("parallel", "arbitrary"),
                                             vmem_limit_bytes=V7X_VMEM_LIMIT_BYTES),
        name="gated_delta_chunked",
    )(qkv, qkv, qkv, z, col, row, norm_w.reshape(1, DN_DH), state0)


def _delta_group(dn_qkv, ab, z, conv0, state0, lw):
    B, T, _ = dn_qkv.shape
    xp = jnp.concatenate([conv0, dn_qkv], axis=1)
    cw = lw['dn_conv_w']
    conv = xp[:, 0:T] * cw[0]
    for i in range(1, CONV_W):
        conv = conv + xp[:, i:i + T] * cw[i]
    conv = _silu(conv)
    new_conv = xp[:, T:]
    dn_a, dn_b = ab[..., :DN_HEADS], ab[..., DN_HEADS:IN_AB]
    beta = jax.nn.sigmoid(dn_b)
    g = -jnp.exp(lw['dn_a_log']) * jax.nn.softplus(dn_a + lw['dn_dt_bias'])
    if T % DN_CHUNK == 0:
        o_dn, new_state = _gated_delta_chunked(conv, z, beta, g, lw['dn_norm_w'], state0)
        return o_dn.reshape(B * T, DN_WIDTH), new_state, new_conv
    dq, dk, dv = jnp.split(conv, 3, axis=-1)
    dq = _l2norm(dq.reshape(B, T, DN_HEADS, DN_DH)) * DN_DH ** -0.5
    dk = _l2norm(dk.reshape(B, T, DN_HEADS, DN_DH))
    dv = dv.reshape(B, T, DN_HEADS, DN_DH)
    o_dn, new_state = _gated_delta(dq, dk, dv, beta, g, state0)
    zz = z.reshape(B, T, DN_HEADS, DN_DH)
    o_dn = o_dn * lax.rsqrt(jnp.mean(o_dn * o_dn, -1, keepdims=True) + 1e-6) * lw['dn_norm_w'] * _silu(zz)
    return o_dn.astype(BF16).reshape(B * T, DN_WIDTH), new_state, new_conv


def _cross_attn(q, mem_k, mem_v):
    B, T, _ = q.shape
    q = q.reshape(B, T, MEM_HEADS, MEM_DH)
    s = jnp.einsum('bthd,bmhd->bhtm', q, mem_k, preferred_element_type=F32) * MEM_DH ** -0.5
    p = jax.nn.softmax(s, axis=-1)
    o = jnp.einsum('bhtm,bmhd->bthd', p, mem_v)
    return o.astype(BF16).reshape(B * T, MEM_WIDTH)


def _mixer_and_cross(x, group, lw, lam, lam_init, alpha):
    B, T, D = x.shape
    n = B * T
    tm = _row_tile(T if group['prompt'] else n)
    qb, kf, vf, kb, vb, dn, z, ab = _in_proj(x.reshape(n, D), lw['w_in'], group['rope'],
                                             (T // tm) if group['prompt'] else 1, tm)
    if group['prompt']:
        o_da = _diff_attn_prompt(qb.reshape(B, T, DA_WIDTH), kb.reshape(B, T, DA_WIDTH),
                                 vb.reshape(B, T, DA_WIDTH), lam, lw['da_norm_w'], 1.0 - lam_init)
        o_da = o_da.reshape(n, DA_WIDTH)
    else:
        o = _diff_attn_sample(qb.reshape(B, T, DA_WIDTH), kb.reshape(B, T, DA_WIDTH), vb.reshape(B, T, DA_WIDTH),
                              lam, group['cache_k'], group['cache_v'], group['page_table'])
        o = o * lax.rsqrt(jnp.mean(o * o, -1, keepdims=True) + LN_EPS) * lw['da_norm_w'] * (1.0 - lam_init)
        o_da = o.astype(BF16).reshape(n, DA_WIDTH)
    o_dn, state, conv = _delta_group(dn.reshape(B, T, 3 * DN_WIDTH), ab.reshape(B, T, LANES),
                                     z.reshape(B, T, DN_WIDTH), group['conv0'], group['state0'], lw)
    x1, x1b = _mm_res_ln([o_da, o_dn], [lw['w_out'][:DA_WIDTH], lw['w_out'][DA_WIDTH:]],
                         x.reshape(n, D), lw['ln1_g'], lw['ln1_b'], alpha)
    qc = _matmul(x1b, lw['w_cq'])
    oc = _cross_attn(qc.reshape(B, T, MEM_WIDTH), group['mem_k'], group['mem_v'])
    x2, x2b, logits = _mm_res_ln([oc], [lw['w_co']], x1, lw['ln2_g'], lw['ln2_b'], alpha, lw['w_router'])
    return (x2, x2b, logits, kf.reshape(B, T, DA_HEADS, DA_QK), vf.reshape(B, T, DA_HEADS, DA_DV), state, conv)


def kernel(x_prompt, x_sample, cache_attn_k, cache_attn_v, cache_mem_k, cache_mem_v, state_dn, state_dn_conv,
           page_table, mem_prompt, w_in, da_lambda_q1, da_lambda_k1, da_lambda_q2, da_lambda_k2, da_norm_w,
           dn_conv_w, dn_a_log, dn_dt_bias, dn_norm_w, w_out, ln1_g, ln1_b, w_cq, w_ckv, w_co, ln2_g, ln2_b,
           w_router, router_bias, w_exp_gu, w_exp_down, w_sh_gu, w_sh_down, ln3_g, ln3_b):
    depth = w_in.shape[0]
    alpha = (2 * depth) ** 0.25
    Bp, Sp, D = x_prompt.shape
    Bs, Ts, _ = x_sample.shape
    past = page_table.shape[1] * cache_attn_k.shape[2]
    rope_p = _rope_tables(jnp.arange(Sp, dtype=jnp.int32))
    rope_s = _rope_tables(jnp.tile(past + jnp.arange(Ts, dtype=jnp.int32), Bs))
    yp, ys = x_prompt, x_sample
    outs = [[] for _ in range(10)]
    for l in range(depth):
        wi = w_in[l]
        w_in_perm = jnp.concatenate(
            [wi[:, :IN_MAIN], wi[:, IN_MAIN + IN_AB:], wi[:, IN_MAIN:IN_MAIN + IN_AB],
             jnp.zeros((D, LANES - IN_AB), wi.dtype)], axis=1).astype(BF16)
        lw = {'w_in': w_in_perm, 'da_norm_w': da_norm_w[l], 'dn_conv_w': dn_conv_w[l], 'dn_a_log': dn_a_log[l],
              'dn_dt_bias': dn_dt_bias[l], 'dn_norm_w': dn_norm_w[l], 'w_out': w_out[l].astype(BF16),
              'ln1_g': ln1_g[l], 'ln1_b': ln1_b[l], 'w_cq': w_cq[l].astype(BF16), 'w_co': w_co[l].astype(BF16),
              'ln2_g': ln2_g[l], 'ln2_b': ln2_b[l], 'w_router': w_router[l].astype(BF16)}
        lam_init = 0.8 - 0.6 * math.exp(-0.3 * l)
        lam = (jnp.exp(jnp.sum(da_lambda_q1[l] * da_lambda_k1[l]))
               - jnp.exp(jnp.sum(da_lambda_q2[l] * da_lambda_k2[l])) + lam_init)
        mkv = _matmul(mem_prompt.reshape(-1, D), w_ckv[l].astype(BF16))
        n_mem = mem_prompt.shape[1]
        mk_p = mkv[:, :MEM_WIDTH].reshape(Bp, n_mem, MEM_HEADS, MEM_DH)
        mv_p = mkv[:, MEM_WIDTH:].reshape(Bp, n_mem, MEM_HEADS, MEM_DH)
        grp_p = {'prompt': True, 'rope': rope_p, 'mem_k': mk_p, 'mem_v': mv_p,
                 'conv0': jnp.zeros((Bp, CONV_W - 1, 3 * DN_WIDTH), F32),
                 'state0': jnp.zeros((Bp, DN_HEADS, DN_DH, DN_DH), F32)}
        grp_s = {'prompt': False, 'rope': rope_s, 'mem_k': cache_mem_k[l], 'mem_v': cache_mem_v[l],
                 'conv0': state_dn_conv[l], 'state0': state_dn[l], 'cache_k': cache_attn_k[l],
                 'cache_v': cache_attn_v[l], 'page_table': page_table}
        x2p, x2bp, lgp, kp, vp, sp, cp = _mixer_and_cross(yp, grp_p, lw, lam, lam_init, alpha)
        x2s, x2bs, lgs, ks_, vs_, ss_, cs_ = _mixer_and_cross(ys, grp_s, lw, lam, lam_init, alpha)
        idx, gate = _route(jnp.concatenate([lgp, lgs], 0), router_bias[l])
        routed_p, routed_s = _routed([x2bp, x2bs], idx, gate, w_exp_gu[l].astype(BF16), w_exp_down[l].astype(BF16))
        wsg, wsd = w_sh_gu[l].astype(BF16), w_sh_down[l].astype(BF16)
        yp = _moe_out(x2p, routed_p, wsg, wsd, ln3_g[l], ln3_b[l], alpha).reshape(Bp, Sp, D)
        ys = _moe_out(x2s, routed_s, wsg, wsd, ln3_g[l], ln3_b[l], alpha).reshape(Bs, Ts, D)
        for lst, val in zip(outs, (kp, vp, ks_, vs_, sp, ss_, cp, cs_, mk_p, mv_p)):
            lst.append(val)
    return (yp, ys) + tuple(jnp.stack(o) for o in outs)
```
